```python
import jax, jax.numpy as jnp
from jax import lax
import numpy as np

D_MODEL = 1024
BATCH = 8
SEQ = 2048
DEPTH = 2

N_MIXERS = 2
N_RGLRU_LAYERS = (DEPTH + 1) // 2
N_DSA_LAYERS = DEPTH // 2
RMS_EPS = 1e-6
ROPE_THETA = 500000.0

LRU_W = D_MODEL
LRU_BLOCKS = 16
LRU_BW = LRU_W // LRU_BLOCKS
LRU_CONV = 4
LRU_C = 8.0

ATT_HEADS = 8
ATT_HD = 128
ATT_W = ATT_HEADS * ATT_HD
ATT_ROT = ATT_HD // 4
IDX_HEADS = 8
IDX_HD = 64
IDX_ROT = IDX_HD // 4
TOPK_MAX = 256
DSA_Q_BLOCK = 32

MEM_TOKENS = 256
MEM_HEADS = 4
MEM_HD = 128
MEM_W = MEM_HEADS * MEM_HD

D_FF = 2816
FFN_CONV = 3

LRU_IN_SIZES = (LRU_W, LRU_W, MEM_W)
DSA_IN_SIZES = (ATT_W, ATT_W, ATT_W, IDX_HEADS * IDX_HD, IDX_HD, IDX_HEADS, MEM_W)
LRU_IN = sum(LRU_IN_SIZES)
DSA_IN = sum(DSA_IN_SIZES)

kernel_name = "hybrid_rglru_dsa_memxattn_convffn"


def split_cols(z, sizes):
    return jnp.split(z, [int(c) for c in np.cumsum(sizes)[:-1]], axis=-1)


def rms_norm(x, g):
    xf = x.astype(jnp.float32)
    y = xf * lax.rsqrt(jnp.mean(xf * xf, axis=-1, keepdims=True) + RMS_EPS)
    return (y * g.astype(jnp.float32)).astype(x.dtype)


def causal_dwconv(x, w, b):
    k_w = w.shape[0]
    s = x.shape[1]
    xp = jnp.pad(x, ((0, 0), (k_w - 1, 0), (0, 0)))
    y = b + xp[:, 0:s] * w[0]
    for k in range(1, k_w):
        y = y + xp[:, k:k + s] * w[k]
    return y


def rope_tables(positions, rot_dim):
    inv = ROPE_THETA ** (-jnp.arange(0, rot_dim, 2, dtype=jnp.float32) / rot_dim)
    ang = positions.astype(jnp.float32)[..., None] * inv
    return jnp.cos(ang), jnp.sin(ang)


def apply_partial_rope(x, cos, sin):
    half = cos.shape[-1]
    c = cos[:, :, None, :].astype(x.dtype)
    s = sin[:, :, None, :].astype(x.dtype)
    x1 = x[..., :half]
    x2 = x[..., half:2 * half]
    return jnp.concatenate([x1 * c - x2 * s, x2 * c + x1 * s, x[..., 2 * half:]], axis=-1)


def memory_attention(q, mem_k, mem_v):
    b, s = q.shape[0], q.shape[1]
    sc = jnp.einsum('bshd,bmhd->bhsm', q, mem_k).astype(jnp.float32) * (MEM_HD ** -0.5)
    p = jax.nn.softmax(sc, axis=-1).astype(mem_v.dtype)
    return jnp.einsum('bhsm,bmhd->bshd', p, mem_v).reshape(b, s, MEM_W)


def rglru_mixer(h, mem_k, mem_v, w_in, conv_w, conv_b, w_a, b_a, w_x, b_x, lam, w_out):
    b, s, _ = h.shape
    xb, gb, mq = split_cols(h @ w_in, LRU_IN_SIZES)
    xc = causal_dwconv(xb, conv_w, conv_b)
    xblk = xc.reshape(b, s, LRU_BLOCKS, LRU_BW)
    r = jax.nn.sigmoid(jnp.einsum('bsnc,ncd->bsnd', xblk, w_a).reshape(b, s, LRU_W) + b_a)
    i = jax.nn.sigmoid(jnp.einsum('bsnc,ncd->bsnd', xblk, w_x).reshape(b, s, LRU_W) + b_x)
    log_a = -LRU_C * r.astype(jnp.float32) * jax.nn.softplus(-lam.astype(jnp.float32))
    a = jnp.exp(log_a)
    gain = jnp.sqrt(-jnp.expm1(2.0 * log_a))
    bt = gain * (i * xc).astype(jnp.float32)

    def combine(left, right):
        a1, b1 = left
        a2, b2 = right
        return a1 * a2, a2 * b1 + b2

    _, hs = lax.associative_scan(combine, (a, bt), axis=1)
    y = hs.astype(h.dtype) * jax.nn.gelu(gb)
    m = memory_attention(mq.reshape(b, s, MEM_HEADS, MEM_HD), mem_k, mem_v)
    return jnp.concatenate([y, m], axis=-1) @ w_out


def dsa_mixer(h, cos_a, sin_a, cos_i, sin_i, mem_k, mem_v, w_in, w_out):
    b, s, _ = h.shape
    q, k, v, iq, ik, iw, mq = split_cols(h @ w_in, DSA_IN_SIZES)
    q = apply_partial_rope(q.reshape(b, s, ATT_HEADS, ATT_HD), cos_a, sin_a)
    k = apply_partial_rope(k.reshape(b, s, ATT_HEADS, ATT_HD), cos_a, sin_a)
    iq = apply_partial_rope(iq.reshape(b, s, IDX_HEADS, IDX_HD), cos_i, sin_i)
    ik = apply_partial_rope(ik[:, :, None, :], cos_i, sin_i)[:, :, 0].astype(jnp.float32)
    iw = iw * (IDX_HEADS ** -0.5)
    topk = min(TOPK_MAX, s // 4)
    n_blk = s // DSA_Q_BLOCK
    k_flat = k.reshape(b, s, ATT_W)
    v_flat = v.reshape(b, s, ATT_W)
    key_pos = jnp.arange(s)
    gather = jax.vmap(lambda arr, ix: arr[ix])

    def to_blocks(t):
        return jnp.moveaxis(t.reshape(b, n_blk, DSA_Q_BLOCK, *t.shape[2:]), 1, 0)

    def attend_block(args):
        qb, iqb, iwb, t0 = args
        qpos = t0 + jnp.arange(DSA_Q_BLOCK)
        rel = jax.nn.relu(jnp.einsum('bqhd,bsd->bqhs', iqb.astype(jnp.float32), ik) * (IDX_HD ** -0.5))
        score = jnp.einsum('bqh,bqhs->bqs', iwb.astype(jnp.float32), rel)
        causal = key_pos[None, :] <= qpos[:, None]
        score = jnp.where(causal[None], score, -jnp.inf)
        _, idx = lax.top_k(score, topk)
        valid = idx <= qpos[None, :, None]
        flat_idx = idx.reshape(b, DSA_Q_BLOCK * topk)
        ks = gather(k_flat, flat_idx).reshape(b, DSA_Q_BLOCK, topk, ATT_HEADS, ATT_HD)
        vs = gather(v_flat, flat_idx).reshape(b, DSA_Q_BLOCK, topk, ATT_HEADS, ATT_HD)
        sc = jnp.einsum('bqhd,bqkhd->bhqk', qb, ks).astype(jnp.float32) * (ATT_HD ** -0.5)
        sc = jnp.where(valid[:, None], sc, -jnp.inf)
        p = jax.nn.softmax(sc, axis=-1).astype(vs.dtype)
        return jnp.einsum('bhqk,bqkhd->bqhd', p, vs)

    out = lax.map(attend_block, (to_blocks(q), to_blocks(iq), to_blocks(iw),
                                 jnp.arange(n_blk) * DSA_Q_BLOCK))
    out = jnp.moveaxis(out, 0, 1).reshape(b, s, ATT_W)
    m = memory_attention(mq.reshape(b, s, MEM_HEADS, MEM_HD), mem_k, mem_v)
    return jnp.concatenate([out, m], axis=-1) @ w_out


def conv_ffn(h, w_up, conv_w, conv_b, w_down):
    u = causal_dwconv(h @ w_up, conv_w, conv_b)
    g, val = jnp.split(u, 2, axis=-1)
    return (jax.nn.silu(g) * val) @ w_down


def setup_inputs(seed: int = 0) -> dict:
    key = jax.random.key(seed)
    ks = jax.random.split(key, 24)
    f32 = jnp.float32

    def nrm(k, shape, fan_in):
        return jax.random.normal(k, shape, f32) * (fan_in ** -0.5)

    def gain(k, shape):
        return 1.0 + 0.01 * jax.random.normal(k, shape, f32)

    def small(k, shape):
        return 0.01 * jax.random.normal(k, shape, f32)

    x = jax.random.normal(ks[0], (BATCH, SEQ, D_MODEL), f32)
    mem = jax.random.normal(ks[1], (BATCH, MEM_TOKENS, D_MODEL), f32)
    offs = jax.random.randint(ks[2], (BATCH, 1), 0, 1024, dtype=jnp.int32)
    positions = offs + jnp.arange(SEQ, dtype=jnp.int32)[None, :]
    a_init = jax.random.uniform(ks[3], (N_RGLRU_LAYERS, LRU_W), f32, 0.9, 0.999)
    s_init = a_init ** (1.0 / LRU_C)
    lru_lambda = jnp.log(s_init) - jnp.log1p(-s_init)
    return {
        "x": x,
        "mem": mem,
        "positions": positions,
        "norm_mix": gain(ks[4], (DEPTH, D_MODEL)),
        "norm_ffn": gain(ks[5], (DEPTH, D_MODEL)),
        "mem_norm": gain(ks[6], (D_MODEL,)),
        "final_norm": gain(ks[7], (D_MODEL,)),
        "w_mem_kv": nrm(ks[8], (DEPTH, D_MODEL, 2 * MEM_W), D_MODEL),
        "w_ffn_up": nrm(ks[9], (DEPTH, D_MODEL, 2 * D_FF), D_MODEL),
        "ffn_conv_w": nrm(ks[10], (DEPTH, FFN_CONV, 2 * D_FF), FFN_CONV),
        "ffn_conv_b": small(ks[11], (DEPTH, 2 * D_FF)),
        "w_ffn_down": nrm(ks[12], (DEPTH, D_FF, D_MODEL), D_FF),
        "lru_w_in": nrm(ks[13], (N_RGLRU_LAYERS, D_MODEL, LRU_IN), D_MODEL),
        "lru_conv_w": nrm(ks[14], (N_RGLRU_LAYERS, LRU_CONV, LRU_W), LRU_CONV),
        "lru_conv_b": small(ks[15], (N_RGLRU_LAYERS, LRU_W)),
        "lru_w_a": nrm(ks[16], (N_RGLRU_LAYERS, LRU_BLOCKS, LRU_BW, LRU_BW), LRU_BW),
        "lru_b_a": small(ks[17], (N_RGLRU_LAYERS, LRU_W)),
        "lru_w_x": nrm(ks[18], (N_RGLRU_LAYERS, LRU_BLOCKS, LRU_BW, LRU_BW), LRU_BW),
        "lru_b_x": small(ks[19], (N_RGLRU_LAYERS, LRU_W)),
        "lru_lambda": lru_lambda,
        "lru_w_out": nrm(ks[20], (N_RGLRU_LAYERS, LRU_W + MEM_W, D_MODEL), LRU_W + MEM_W),
        "dsa_w_in": nrm(ks[21], (N_DSA_LAYERS, D_MODEL, DSA_IN), D_MODEL),
        "dsa_w_out": nrm(ks[22], (N_DSA_LAYERS, ATT_W + MEM_W, D_MODEL), ATT_W + MEM_W),
    }


def reference(x, mem, positions, norm_mix, norm_ffn, mem_norm, final_norm, w_mem_kv,
              w_ffn_up, ffn_conv_w, ffn_conv_b, w_ffn_down,
              lru_w_in, lru_conv_w, lru_conv_b, lru_w_a, lru_b_a, lru_w_x, lru_b_x,
              lru_lambda, lru_w_out, dsa_w_in, dsa_w_out):
    b = x.shape[0]
    cos_a, sin_a = rope_tables(positions, ATT_ROT)
    cos_i, sin_i = rope_tables(positions, IDX_ROT)
    memn = rms_norm(mem, mem_norm)
    for i in range(DEPTH):
        mk, mv = jnp.split(memn @ w_mem_kv[i], 2, axis=-1)
        mk = mk.reshape(b, MEM_TOKENS, MEM_HEADS, MEM_HD)
        mv = mv.reshape(b, MEM_TOKENS, MEM_HEADS, MEM_HD)
        hn = rms_norm(x, norm_mix[i])
        j = i // N_MIXERS
        if i % N_MIXERS == 0:
            x = x + rglru_mixer(hn, mk, mv, lru_w_in[j], lru_conv_w[j], lru_conv_b[j],
                                lru_w_a[j], lru_b_a[j], lru_w_x[j], lru_b_x[j],
                                lru_lambda[j], lru_w_out[j])
        else:
            x = x + dsa_mixer(hn, cos_a, sin_a, cos_i, sin_i, mk, mv, dsa_w_in[j], dsa_w_out[j])
        x = x + conv_ffn(rms_norm(x, norm_ffn[i]), w_ffn_up[i], ffn_conv_w[i], ffn_conv_b[i],
                         w_ffn_down[i])
    return rms_norm(x, final_norm)
```

```python
import functools

import numpy as np
import jax
import jax.numpy as jnp
from jax import lax
from jax.experimental import pallas as pl
from jax.experimental.pallas import tpu as pltpu

F32 = jnp.float32
BF16 = jnp.bfloat16
I32 = jnp.int32

D_MODEL = 1024
RMS_EPS = 1e-6
ROPE_THETA = 500000.0

LRU_W = 1024
LRU_BLOCKS = 16
LRU_BW = LRU_W // LRU_BLOCKS
LRU_CONV = 4
LRU_C = 8.0

ATT_HEADS = 8
ATT_HD = 128
ATT_W = ATT_HEADS * ATT_HD
ATT_ROT = ATT_HD // 4
IDX_HEADS = 8
IDX_HD = 64
IDX_ROT = IDX_HD // 4
TOPK_MAX = 256

MEM_TOKENS = 256
MEM_HEADS = 4
MEM_HD = 128
MEM_W = MEM_HEADS * MEM_HD

D_FF = 2816
FFN_CONV = 3

SUBLANES = 8
LANES = 128
MXU_TILE = 256
VMEM_LIMIT_BYTES = 56 * 1024 * 1024

INT_MIN = np.int32(-2 ** 31)
NEG_BIG = -1e30

TOK_TILE = 512
LRU_TILE = 256
Q_TILE = 256
K_CHUNK = 256
FF_CHUNK = 256


def _params(sem):
    return pltpu.CompilerParams(dimension_semantics=sem, vmem_limit_bytes=VMEM_LIMIT_BYTES)


def _rms(x, g):
    ms = jnp.mean(x * x, axis=-1, keepdims=True)
    return x * lax.rsqrt(ms + RMS_EPS) * g


def _shift_rows(x, prev, j):
    r = pltpu.roll(x, j, 0)
    p = pltpu.roll(prev, j, 0)
    rid = lax.broadcasted_iota(I32, (SUBLANES, x.shape[1]), 0)
    top = jnp.where(rid < j, p, r[0:SUBLANES])
    return jnp.concatenate([top, r[SUBLANES:]], axis=0)


def _norm_proj_kernel(x_ref, g_ref, w_ref, *out_refs, splits):
    hn = _rms(x_ref[...], g_ref[...]).astype(BF16)
    z = jnp.dot(hn, w_ref[...], preferred_element_type=F32)
    off = 0
    for o_ref, n in zip(out_refs, splits):
        o_ref[...] = z[:, off:off + n].astype(o_ref.dtype)
        off += n


def _norm_proj(x, g, w, splits, dtypes, name):
    t, d = x.shape
    n = w.shape[1]
    tm = min(TOK_TILE, t)
    return pl.pallas_call(
        functools.partial(_norm_proj_kernel, splits=splits),
        grid=(t // tm,),
        in_specs=[
            pl.BlockSpec((tm, d), lambda i: (i, 0)),
            pl.BlockSpec((1, d), lambda i: (0, 0)),
            pl.BlockSpec((d, n), lambda i: (0, 0)),
        ],
        out_specs=[pl.BlockSpec((tm, s), lambda i: (i, 0)) for s in splits],
        out_shape=[jax.ShapeDtypeStruct((t, s), dt) for s, dt in zip(splits, dtypes)],
        compiler_params=_params(("arbitrary",)),
        name=name,
    )(x, g.reshape(1, d), w)


DSA_SPLITS = (ATT_W, ATT_W, ATT_W, IDX_HEADS * IDX_HD, LANES, LANES, MEM_W)


def _rope(xh, c, s_lo, s_hi, half):
    return xh * c + pltpu.roll(xh, LANES - half, 1) * s_lo + pltpu.roll(xh, half, 1) * s_hi


def _dsa_proj_kernel(x_ref, g_ref, w_ref, pos_ref, inva_ref, invi_ref,
                     q_ref, k_ref, v_ref, iq_ref, ik_ref, iw_ref, mq_ref):
    hn = _rms(x_ref[...], g_ref[...]).astype(BF16)
    z = jnp.dot(hn, w_ref[...], preferred_element_type=F32)
    tm = z.shape[0]
    pos = pos_ref[...]
    lane = lax.broadcasted_iota(I32, (tm, LANES), 1)
    ang_a = pos * inva_ref[...]
    ca, sa = jnp.cos(ang_a), jnp.sin(ang_a)
    ha = ATT_ROT // 2
    sa_lo = jnp.where(lane < ha, -sa, 0.0)
    sa_hi = jnp.where((lane >= ha) & (lane < 2 * ha), sa, 0.0)
    ang_i = pos * invi_ref[...]
    ci, si = jnp.cos(ang_i), jnp.sin(ang_i)
    hi = IDX_ROT // 2
    m64 = lane & (IDX_HD - 1)
    si_lo = jnp.where(m64 < hi, -si, 0.0)
    si_hi = jnp.where((m64 >= hi) & (m64 < 2 * hi), si, 0.0)

    off = 0
    for h in range(ATT_HEADS):
        sl = slice(off + h * ATT_HD, off + (h + 1) * ATT_HD)
        q_ref[:, h * ATT_HD:(h + 1) * ATT_HD] = _rope(z[:, sl], ca, sa_lo, sa_hi, ha).astype(q_ref.dtype)
    off += ATT_W
    for h in range(ATT_HEADS):
        sl = slice(off + h * ATT_HD, off + (h + 1) * ATT_HD)
        k_ref[:, h * ATT_HD:(h + 1) * ATT_HD] = _rope(z[:, sl], ca, sa_lo, sa_hi, ha).astype(k_ref.dtype)
    off += ATT_W
    v_ref[...] = z[:, off:off + ATT_W].astype(v_ref.dtype)
    off += ATT_W
    for p in range(IDX_HEADS * IDX_HD // LANES):
        sl = slice(off + p * LANES, off + (p + 1) * LANES)
        iq_ref[:, p * LANES:(p + 1) * LANES] = _rope(z[:, sl], ci, si_lo, si_hi, hi).astype(iq_ref.dtype)
    off += IDX_HEADS * IDX_HD
    ik_ref[...] = _rope(z[:, off:off + LANES], ci, si_lo, si_hi, hi).astype(ik_ref.dtype)
    off += LANES
    iw_ref[...] = z[:, off:off + LANES].astype(iw_ref.dtype)
    off += LANES
    mq_ref[...] = z[:, off:off + MEM_W].astype(mq_ref.dtype)


def _dsa_proj(x, g, w, pos, inva, invi):
    t, d = x.shape
    n = w.shape[1]
    tm = TOK_TILE
    dtypes = (BF16, BF16, BF16, BF16, BF16, F32, BF16)
    return pl.pallas_call(
        _dsa_proj_kernel,
        grid=(t // tm,),
        in_specs=[
            pl.BlockSpec((tm, d), lambda i: (i, 0)),
            pl.BlockSpec((1, d), lambda i: (0, 0)),
            pl.BlockSpec((d, n), lambda i: (0, 0)),
            pl.BlockSpec((tm, 1), lambda i: (i, 0)),
            pl.BlockSpec((1, LANES), lambda i: (0, 0)),
            pl.BlockSpec((1, LANES), lambda i: (0, 0)),
        ],
        out_specs=[pl.BlockSpec((tm, s), lambda i: (i, 0)) for s in DSA_SPLITS],
        out_shape=[jax.ShapeDtypeStruct((t, s), dt) for s, dt in zip(DSA_SPLITS, dtypes)],
        compiler_params=_params(("arbitrary",)),
        name="dsa_proj",
    )(x, g.reshape(1, d), w, pos, inva, invi)


def _lru_kernel(xb_ref, gb_ref, cw_ref, cb_ref, wa_ref, ba_ref, wx_ref, bx_ref, lam_ref, y_ref,
                prev_ref, hc_ref, a_s, b_s):
    ts, c = xb_ref.shape

    @pl.when(pl.program_id(1) == 0)
    def _():
        prev_ref[...] = jnp.zeros_like(prev_ref)
        hc_ref[...] = jnp.zeros_like(hc_ref)

    x = xb_ref[...]
    prev = prev_ref[...]
    xc = cb_ref[...] + cw_ref[3:4, :] * x
    for j in range(1, LRU_CONV):
        xc = xc + cw_ref[LRU_CONV - 1 - j:LRU_CONV - j, :] * _shift_rows(x, prev, j)
    prev_ref[...] = x[ts - SUBLANES:ts, :]

    xcb = xc.astype(BF16)
    ga, gx = [], []
    for t in range(c // MXU_TILE):
        blk = xcb[:, t * MXU_TILE:(t + 1) * MXU_TILE]
        ga.append(jnp.dot(blk, wa_ref[t], preferred_element_type=F32))
        gx.append(jnp.dot(blk, wx_ref[t], preferred_element_type=F32))
    r = jax.nn.sigmoid(jnp.concatenate(ga, axis=1) + ba_ref[...])
    i = jax.nn.sigmoid(jnp.concatenate(gx, axis=1) + bx_ref[...])

    nl = -lam_ref[...]
    softplus = jnp.maximum(nl, 0.0) + jnp.log1p(jnp.exp(-jnp.abs(nl)))
    log_a = (-LRU_C) * r * softplus
    a = jnp.exp(log_a)
    gain = jnp.sqrt(-jnp.tanh(log_a) * (a * a + 1.0))
    bt = gain * (i * xc)

    a3 = a.reshape(ts // SUBLANES, SUBLANES, c)
    b3 = bt.reshape(ts // SUBLANES, SUBLANES, c)
    rid = lax.broadcasted_iota(I32, (1, SUBLANES, c), 1)
    for d in (1, 2, 4):
        a_sh = jnp.where(rid >= d, pltpu.roll(a3, d, 1), 1.0)
        b_sh = jnp.where(rid >= d, pltpu.roll(b3, d, 1), 0.0)
        b3 = a3 * b_sh + b3
        a3 = a3 * a_sh
    a_s[...] = a3.reshape(ts, c)
    b_s[...] = b3.reshape(ts, c)

    def body(g, hc):
        r0 = pl.multiple_of(g * SUBLANES, SUBLANES)
        h = a_s[pl.ds(r0, SUBLANES), :] * hc + b_s[pl.ds(r0, SUBLANES), :]
        b_s[pl.ds(r0, SUBLANES), :] = h
        return jnp.broadcast_to(h[SUBLANES - 1:SUBLANES, :], (SUBLANES, c))

    hc_ref[...] = lax.fori_loop(0, ts // SUBLANES, body, hc_ref[...])
    y_ref[...] = (b_s[...] * jax.nn.gelu(gb_ref[...])).astype(y_ref.dtype)


def _lru_core(xb, gb, cw, cb, wa, ba, wx, bx, lam, batch, seq):
    c = LRU_W
    ts = LRU_TILE
    nt = seq // ts
    row = lambda b, j: (b * nt + j, 0)
    const2 = lambda b, j: (0, 0)
    const3 = lambda b, j: (0, 0, 0)
    return pl.pallas_call(
        _lru_kernel,
        grid=(batch, nt),
        in_specs=[
            pl.BlockSpec((ts, c), row),
            pl.BlockSpec((ts, c), row),
            pl.BlockSpec((LRU_CONV, c), const2),
            pl.BlockSpec((1, c), const2),
            pl.BlockSpec((c // MXU_TILE, MXU_TILE, MXU_TILE), const3),
            pl.BlockSpec((1, c), const2),
            pl.BlockSpec((c // MXU_TILE, MXU_TILE, MXU_TILE), const3),
            pl.BlockSpec((1, c), const2),
            pl.BlockSpec((1, c), const2),
        ],
        out_specs=pl.BlockSpec((ts, c), row),
        out_shape=jax.ShapeDtypeStruct((batch * seq, c), BF16),
        scratch_shapes=[
            pltpu.VMEM((SUBLANES, c), F32),
            pltpu.VMEM((SUBLANES, c), F32),
            pltpu.VMEM((ts, c), F32),
            pltpu.VMEM((ts, c), F32),
        ],
        compiler_params=_params(("arbitrary", "arbitrary")),
        name="lru_core",
    )(xb, gb, cw, cb.reshape(1, c), wa, ba.reshape(1, c), wx, bx.reshape(1, c), lam.reshape(1, c))


def _block_diag_tiles(w):
    per = MXU_TILE // LRU_BW
    w4 = w.reshape(LRU_BLOCKS // per, per, LRU_BW, LRU_BW)
    eye = jnp.eye(per, dtype=w.dtype)
    t = jnp.einsum('gpij,pq->gpiqj', w4, eye)
    return t.reshape(LRU_BLOCKS // per, MXU_TILE, MXU_TILE)


def _mix_out_kernel(a_ref, mq_ref, mk_ref, mv_ref, w_ref, x_ref, o_ref):
    mq = mq_ref[...]
    mk = mk_ref[...]
    mv = mv_ref[...]
    scale = MEM_HD ** -0.5
    heads = []
    for h in range(MEM_HEADS):
        sl = slice(h * MEM_HD, (h + 1) * MEM_HD)
        s = lax.dot_general(mq[:, sl], mk[:, sl], (((1,), (1,)), ((), ())),
                            preferred_element_type=F32) * scale
        s = s - jnp.max(s, axis=-1, keepdims=True)
        e = jnp.exp(s)
        p = e / jnp.sum(e, axis=-1, keepdims=True)
        heads.append(jnp.dot(p.astype(BF16), mv[:, sl], preferred_element_type=F32).astype(BF16))
    cat = jnp.concatenate([a_ref[...]] + heads, axis=1)
    o_ref[...] = x_ref[...] + jnp.dot(cat, w_ref[...], preferred_element_type=F32)


def _mix_out(a, mq, memkv, layer, w_out, x, seq):
    t, d = x.shape
    tm = TOK_TILE
    per = seq // tm
    kin = w_out.shape[0]
    return pl.pallas_call(
        _mix_out_kernel,
        grid=(t // tm,),
        in_specs=[
            pl.BlockSpec((tm, a.shape[1]), lambda i: (i, 0)),
            pl.BlockSpec((tm, MEM_W), lambda i: (i, 0)),
            pl.BlockSpec((MEM_TOKENS, MEM_W), lambda i: (i // per, 2 * layer)),
            pl.BlockSpec((MEM_TOKENS, MEM_W), lambda i: (i // per, 2 * layer + 1)),
            pl.BlockSpec((kin, d), lambda i: (0, 0)),
            pl.BlockSpec((tm, d), lambda i: (i, 0)),
        ],
        out_specs=pl.BlockSpec((tm, d), lambda i: (i, 0)),
        out_shape=jax.ShapeDtypeStruct((t, d), F32),
        compiler_params=_params(("arbitrary",)),
        name="mix_out",
    )(a, mq, memkv, memkv, w_out, x)


def _ffn_kernel(x_ref, g_ref, wup_ref, cw_ref, wdn_ref, fg_ref, o_ref, hn_s, acc_s, uprev_s, *, final_norm):
    tm = x_ref.shape[0]
    nch = wdn_ref.shape[0]

    @pl.when(pl.program_id(1) == 0)
    def _():
        uprev_s[...] = jnp.zeros_like(uprev_s)

    hn_s[...] = _rms(x_ref[...], g_ref[...]).astype(BF16)
    acc_s[...] = jnp.zeros_like(acc_s)

    def conv_part(idx):
        u = jnp.dot(hn_s[...], wup_ref[idx], preferred_element_type=F32)
        prev = uprev_s[idx]
        cw = cw_ref[idx]
        y = cw[3:4, :] + cw[2:3, :] * u
        y = y + cw[1:2, :] * _shift_rows(u, prev, 1)
        y = y + cw[0:1, :] * _shift_rows(u, prev, 2)
        uprev_s[idx] = u[tm - SUBLANES:tm, :]
        return y

    def body(c, carry):
        gate = conv_part(c)
        val = conv_part(c + nch)
        act = (jax.nn.silu(gate) * val).astype(BF16)
        acc_s[...] += jnp.dot(act, wdn_ref[c], preferred_element_type=F32)
        return carry

    lax.fori_loop(0, nch, body, 0)
    out = x_ref[...] + acc_s[...]
    if final_norm:
        out = _rms(out, fg_ref[...])
    o_ref[...] = out


def _ffn(x, g, wup3, cw3, wdn3, fg, batch, seq, final_norm):
    t, d = x.shape
    tm = TOK_TILE
    nt = seq // tm
    nch = wdn3.shape[0]
    row = lambda b, j: (b * nt + j, 0)
    return pl.pallas_call(
        functools.partial(_ffn_kernel, final_norm=final_norm),
        grid=(batch, nt),
        in_specs=[
            pl.BlockSpec((tm, d), row),
            pl.BlockSpec((1, d), lambda b, j: (0, 0)),
            pl.BlockSpec(wup3.shape, lambda b, j: (0, 0, 0)),
            pl.BlockSpec(cw3.shape, lambda b, j: (0, 0, 0)),
            pl.BlockSpec(wdn3.shape, lambda b, j: (0, 0, 0)),
            pl.BlockSpec((1, d), lambda b, j: (0, 0)),
        ],
        out_specs=pl.BlockSpec((tm, d), row),
        out_shape=jax.ShapeDtypeStruct((t, d), F32),
        scratch_shapes=[
            pltpu.VMEM((tm, d), BF16),
            pltpu.VMEM((tm, d), F32),
            pltpu.VMEM((2 * nch, SUBLANES, FF_CHUNK), F32),
        ],
        compiler_params=_params(("arbitrary", "arbitrary")),
        name="ffn",
    )(x, g.reshape(1, d), wup3, cw3, wdn3, fg.reshape(1, d))


def _ffn_weights(w_up, conv_w, conv_b, w_down):
    d = w_up.shape[0]
    n2 = w_up.shape[1]
    nch2 = n2 // FF_CHUNK
    wup3 = w_up.astype(BF16).reshape(d, nch2, FF_CHUNK).transpose(1, 0, 2)
    cw = jnp.concatenate([conv_w, conv_b[None, :],
                          jnp.zeros((SUBLANES - FFN_CONV - 1, n2), F32)], axis=0)
    cw3 = cw.reshape(SUBLANES, nch2, FF_CHUNK).transpose(1, 0, 2)
    wdn3 = w_down.astype(BF16).reshape(nch2 // 2, FF_CHUNK, d)
    return wup3, cw3, wdn3


def _dsa_attn_kernel(ik_ref, iq_ref, iw_ref, q_ref, k_ref, v_ref, o_ref,
                     keys_s, bias_s, m_s, l_s, acc_s):
    tq = q_ref.shape[0]
    kc_n = K_CHUNK
    qi = pl.program_id(1)
    nk = qi + 1
    grp = kc_n // SUBLANES

    iw_t = iw_ref[...].T[0:IDX_HEADS, :] * ((IDX_HEADS ** -0.5) * (IDX_HD ** -0.5))
    iq = iq_ref[...]
    lane = lax.broadcasted_iota(I32, (kc_n, LANES), 1)
    krow = lax.broadcasted_iota(I32, (kc_n, tq), 0)
    qcol = lax.broadcasted_iota(I32, (kc_n, tq), 1)

    def score_body(kc, carry):
        r0 = pl.multiple_of(kc * kc_n, kc_n)
        ik2 = ik_ref[pl.ds(r0, kc_n), :]
        ik_lo = jnp.where(lane < IDX_HD, ik2, jnp.zeros_like(ik2))
        ik_hi = jnp.where(lane >= IDX_HD, ik2, jnp.zeros_like(ik2))
        score = jnp.zeros((kc_n, tq), F32)
        for h in range(IDX_HEADS):
            pair = iq[:, (h // 2) * LANES:(h // 2 + 1) * LANES]
            lhs = ik_lo if h % 2 == 0 else ik_hi
            rel = lax.dot_general(lhs, pair, (((1,), (1,)), ((), ())), preferred_element_type=F32)
            score = score + iw_t[h:h + 1, :] * jnp.maximum(rel, 0.0)
        causal = (krow + r0) <= (qcol + qi * tq)
        keys_s[pl.ds(r0, kc_n), :] = jnp.where(causal, score, -jnp.inf)
        return carry

    lax.fori_loop(0, nk, score_body, 0)

    def count(pred_fn):
        def body(kc, part):
            r0 = pl.multiple_of(kc * kc_n, kc_n)
            ones = jnp.where(pred_fn(keys_s[pl.ds(r0, kc_n), :], r0), 1, 0).astype(I32)
            return part + jnp.sum(ones.reshape(grp, SUBLANES, tq), axis=0)
        part = lax.fori_loop(0, nk, body, jnp.zeros((SUBLANES, tq), I32))
        return jnp.sum(part, axis=0, keepdims=True)

    def ordered_to_f32(u):
        neg_inf_u = jnp.int32(0x007FFFFF)
        u = jnp.where((u >= 0) & (u < neg_inf_u), neg_inf_u, u)
        k = u ^ INT_MIN
        return pltpu.bitcast(k ^ ((k >> 31) & jnp.int32(0x7FFFFFFF)), F32)

    def bit_body(t, tu):
        cand_u = tu | lax.shift_left(jnp.int32(1), jnp.int32(31) - t)
        cand_f = ordered_to_f32(cand_u)
        cnt = count(lambda kv, r0: kv >= cand_f)
        return jnp.where(cnt >= TOPK_MAX, cand_u, tu)

    tu = lax.fori_loop(0, 32, bit_body, jnp.zeros((1, tq), I32))
    thr = ordered_to_f32(tu)
    short = thr == -jnp.inf
    cnt_gt = count(lambda kv, r0: kv > thr)
    cnt_ge = count(lambda kv, r0: kv >= thr)
    need = TOPK_MAX - cnt_gt
    excess = jnp.where(short, 0, cnt_ge - TOPK_MAX)

    def tie_search():
        def jbit(t, jc):
            cand = jc | lax.shift_left(jnp.int32(1), jnp.int32(11) - t)
            cnt = count(lambda kv, r0: (kv == thr) & ((krow + r0) < cand))
            return jnp.where(cnt <= need, cand, jc)
        return lax.fori_loop(0, 12, jbit, jnp.zeros((1, tq), I32))

    jcut = lax.cond(jnp.max(excess) > 0, tie_search, lambda: jnp.full((1, tq), 4095, I32))

    def bias_body(kc, carry):
        r0 = pl.multiple_of(kc * kc_n, kc_n)
        kv = keys_s[pl.ds(r0, kc_n), :]
        sel = ((kv > thr) | ((kv == thr) & ((krow + r0) < jcut))) & (kv > -jnp.inf)
        bias_s[kc] = jnp.where(sel, 0.0, NEG_BIG).astype(F32).T
        return carry

    lax.fori_loop(0, nk, bias_body, 0)

    scale = ATT_HD ** -0.5
    for h in range(ATT_HEADS):
        sl = slice(h * ATT_HD, (h + 1) * ATT_HD)
        q_h = q_ref[:, sl]
        m_s[...] = jnp.full_like(m_s, NEG_BIG)
        l_s[...] = jnp.zeros_like(l_s)
        acc_s[...] = jnp.zeros_like(acc_s)

        def att_body(kc, carry, sl=sl, q_h=q_h):
            r0 = pl.multiple_of(kc * kc_n, kc_n)
            k_c = k_ref[pl.ds(r0, kc_n), sl]
            v_c = v_ref[pl.ds(r0, kc_n), sl]
            s = lax.dot_general(q_h, k_c, (((1,), (1,)), ((), ())),
                                preferred_element_type=F32) * scale + bias_s[kc]
            m_prev = m_s[...]
            m_new = jnp.maximum(m_prev, jnp.max(s, axis=-1, keepdims=True))
            alpha = jnp.exp(m_prev - m_new)
            p = jnp.exp(s - m_new)
            l_s[...] = alpha * l_s[...] + jnp.sum(p, axis=-1, keepdims=True)
            acc_s[...] = alpha * acc_s[...] + jnp.dot(p.astype(BF16), v_c, preferred_element_type=F32)
            m_s[...] = m_new
            return carry

        lax.fori_loop(0, nk, att_body, 0)
        o_ref[:, sl] = (acc_s[...] / l_s[...]).astype(o_ref.dtype)


def _dsa_attn(ik2, iq, iw, q, k, v, batch, seq):
    tq = Q_TILE
    nq = seq // tq
    r3 = lambda a: a.reshape(batch, seq, a.shape[-1])
    tile = lambda b, j: (b, j, 0)
    full = lambda b, j: (b, 0, 0)
    out = pl.pallas_call(
        _dsa_attn_kernel,
        grid=(batch, nq),
        in_specs=[
            pl.BlockSpec((None, seq, LANES), full),
            pl.BlockSpec((None, tq, IDX_HEADS * IDX_HD), tile),
            pl.BlockSpec((None, tq, LANES), tile),
            pl.BlockSpec((None, tq, ATT_W), tile),
            pl.BlockSpec((None, seq, ATT_W), full),
            pl.BlockSpec((None, seq, ATT_W), full),
        ],
        out_specs=pl.BlockSpec((None, tq, ATT_W), tile),
        out_shape=jax.ShapeDtypeStruct((batch, seq, ATT_W), BF16),
        scratch_shapes=[
            pltpu.VMEM((seq, tq), F32),
            pltpu.VMEM((seq // K_CHUNK, tq, K_CHUNK), F32),
            pltpu.VMEM((tq, 1), F32),
            pltpu.VMEM((tq, 1), F32),
            pltpu.VMEM((tq, ATT_HD), F32),
        ],
        compiler_params=_params(("arbitrary", "arbitrary")),
        name="dsa_attn",
    )(r3(ik2), r3(iq), r3(iw), r3(q), r3(k), r3(v))
    return out.reshape(batch * seq, ATT_W)


def _rope_inv_rows():
    inv_a = ROPE_THETA ** (-jnp.arange(0, ATT_ROT, 2, dtype=F32) / ATT_ROT)
    inv_i = ROPE_THETA ** (-jnp.arange(0, IDX_ROT, 2, dtype=F32) / IDX_ROT)
    row_a = jnp.concatenate([inv_a, inv_a, jnp.zeros((ATT_HD - ATT_ROT,), F32)])
    half = jnp.concatenate([inv_i, inv_i, jnp.zeros((IDX_HD - IDX_ROT,), F32)])
    row_i = jnp.concatenate([half, half])
    return row_a.reshape(1, LANES), row_i.reshape(1, LANES)


def kernel(x, mem, positions, norm_mix, norm_ffn, mem_norm, final_norm, w_mem_kv, w_ffn_up, ffn_conv_w,
           ffn_conv_b, w_ffn_down, lru_w_in, lru_conv_w, lru_conv_b, lru_w_a, lru_b_a, lru_w_x, lru_b_x,
           lru_lambda, lru_w_out, dsa_w_in, dsa_w_out):
    batch, seq, d = x.shape
    t = batch * seq
    xf = x.reshape(t, d)

    w_kv = jnp.concatenate([w_mem_kv[0], w_mem_kv[1]], axis=1).astype(BF16)
    (memkv,) = _norm_proj(mem.reshape(batch * MEM_TOKENS, d), mem_norm, w_kv,
                          (w_kv.shape[1],), (BF16,), "mem_kv")

    xb, gb, mq = _norm_proj(xf, norm_mix[0], lru_w_in[0].astype(BF16),
                            (LRU_W, LRU_W, MEM_W), (F32, F32, BF16), "lru_proj")
    y = _lru_core(xb, gb, lru_conv_w[0], lru_conv_b[0],
                  _block_diag_tiles(lru_w_a[0]).astype(BF16), lru_b_a[0],
                  _block_diag_tiles(lru_w_x[0]).astype(BF16), lru_b_x[0], lru_lambda[0], batch, seq)
    xf = _mix_out(y, mq, memkv, 0, lru_w_out[0].astype(BF16), xf, seq)
    xf = _ffn(xf, norm_ffn[0], *_ffn_weights(w_ffn_up[0], ffn_conv_w[0], ffn_conv_b[0], w_ffn_down[0]),
              final_norm, batch, seq, False)

    w = dsa_w_in[0]
    o = np.cumsum((0, ATT_W, ATT_W, ATT_W, IDX_HEADS * IDX_HD, IDX_HD, IDX_HEADS, MEM_W))
    w_ik = w[:, o[4]:o[5]]
    w_iw = jnp.pad(w[:, o[5]:o[6]], ((0, 0), (0, LANES - IDX_HEADS)))
    w_cat = jnp.concatenate([w[:, :o[4]], w_ik, w_ik, w_iw, w[:, o[6]:o[7]]], axis=1).astype(BF16)
    inva, invi = _rope_inv_rows()
    pos = positions.astype(F32).reshape(t, 1)
    q, k, v, iq, ik2, iw, mq = _dsa_proj(xf, norm_mix[1], w_cat, pos, inva, invi)
    att = _dsa_attn(ik2, iq, iw, q, k, v, batch, seq)
    xf = _mix_out(att, mq, memkv, 1, dsa_w_out[0].astype(BF16), xf, seq)
    xf = _ffn(xf, norm_ffn[1], *_ffn_weights(w_ffn_up[1], ffn_conv_w[1], ffn_conv_b[1], w_ffn_down[1]),
              final_norm, batch, seq, True)
    return xf.reshape(batch, seq, d)
```

```python
import functools

import numpy as np
import jax
import jax.numpy as jnp
from jax import lax
from jax.experimental import pallas as pl
from jax.experimental.pallas import tpu as pltpu

F32 = jnp.float32
BF16 = jnp.bfloat16
I32 = jnp.int32

D_MODEL = 1024
RMS_EPS = 1e-6
ROPE_THETA = 500000.0

LRU_W = 1024
LRU_BLOCKS = 16
LRU_BW = LRU_W // LRU_BLOCKS
LRU_CONV = 4
LRU_C = 8.0

ATT_HEADS = 8
ATT_HD = 128
ATT_W = ATT_HEADS * ATT_HD
ATT_ROT = ATT_HD // 4
IDX_HEADS = 8
IDX_HD = 64
IDX_ROT = IDX_HD // 4
TOPK_MAX = 256

MEM_TOKENS = 256
MEM_HEADS = 4
MEM_HD = 128
MEM_W = MEM_HEADS * MEM_HD

D_FF = 2816
FFN_CONV = 3

SUBLANES = 8
LANES = 128
MXU_TILE = 256
VMEM_LIMIT_BYTES = 56 * 1024 * 1024

INT_MIN = np.int32(-2 ** 31)
NEG_BIG = -1e30
LOG2_E = 1.4426950408889634

TOK_TILE = 512
LRU_TILE = 256
Q_TILE = 256
K_CHUNK = 256
FF_CHUNK = 256


def _params(sem):
    return pltpu.CompilerParams(dimension_semantics=sem, vmem_limit_bytes=VMEM_LIMIT_BYTES)


def _rms(x, g):
    ms = jnp.mean(x * x, axis=-1, keepdims=True)
    return x * lax.rsqrt(ms + RMS_EPS) * g


def _shift_rows(x, prev, j):
    r = pltpu.roll(x, j, 0)
    p = pltpu.roll(prev, j, 0)
    rid = lax.broadcasted_iota(I32, (SUBLANES, x.shape[1]), 0)
    top = jnp.where(rid < j, p, r[0:SUBLANES])
    return jnp.concatenate([top, r[SUBLANES:]], axis=0)


def _norm_proj_kernel(x_ref, g_ref, w_ref, *out_refs, splits):
    hn = _rms(x_ref[...], g_ref[...]).astype(BF16)
    z = jnp.dot(hn, w_ref[...], preferred_element_type=F32)
    off = 0
    for o_ref, n in zip(out_refs, splits):
        o_ref[...] = z[:, off:off + n].astype(o_ref.dtype)
        off += n


def _norm_proj(x, g, w, splits, dtypes, name):
    t, d = x.shape
    n = w.shape[1]
    tm = min(TOK_TILE, t)
    return pl.pallas_call(
        functools.partial(_norm_proj_kernel, splits=splits),
        grid=(t // tm,),
        in_specs=[
            pl.BlockSpec((tm, d), lambda i: (i, 0)),
            pl.BlockSpec((1, d), lambda i: (0, 0)),
            pl.BlockSpec((d, n), lambda i: (0, 0)),
        ],
        out_specs=[pl.BlockSpec((tm, s), lambda i: (i, 0)) for s in splits],
        out_shape=[jax.ShapeDtypeStruct((t, s), dt) for s, dt in zip(splits, dtypes)],
        compiler_params=_params(("arbitrary",)),
        name=name,
    )(x, g.reshape(1, d), w)


DSA_SPLITS = (ATT_W, ATT_W, ATT_W, IDX_HEADS * IDX_HD, LANES, LANES, MEM_W)


def _rope(xh, c, s_lo, s_hi, half):
    return xh * c + pltpu.roll(xh, LANES - half, 1) * s_lo + pltpu.roll(xh, half, 1) * s_hi


def _dsa_proj_kernel(x_ref, g_ref, w_ref, pos_ref, inva_ref, invi_ref,
                     q_ref, k_ref, v_ref, iq_ref, ik_ref, iw_ref, mq_ref):
    hn = _rms(x_ref[...], g_ref[...]).astype(BF16)
    z = jnp.dot(hn, w_ref[...], preferred_element_type=F32)
    tm = z.shape[0]
    pos = pos_ref[...]
    lane = lax.broadcasted_iota(I32, (tm, LANES), 1)
    ang_a = pos * inva_ref[...]
    ca, sa = jnp.cos(ang_a), jnp.sin(ang_a)
    ha = ATT_ROT // 2
    sa_lo = jnp.where(lane < ha, -sa, 0.0)
    sa_hi = jnp.where((lane >= ha) & (lane < 2 * ha), sa, 0.0)
    ang_i = pos * invi_ref[...]
    ci, si = jnp.cos(ang_i), jnp.sin(ang_i)
    hi = IDX_ROT // 2
    m64 = lane & (IDX_HD - 1)
    si_lo = jnp.where(m64 < hi, -si, 0.0)
    si_hi = jnp.where((m64 >= hi) & (m64 < 2 * hi), si, 0.0)

    off = 0
    for h in range(ATT_HEADS):
        sl = slice(off + h * ATT_HD, off + (h + 1) * ATT_HD)
        qh = _rope(z[:, sl], ca, sa_lo, sa_hi, ha) * (ATT_HD ** -0.5 * LOG2_E)
        q_ref[:, h * ATT_HD:(h + 1) * ATT_HD] = qh.astype(q_ref.dtype)
    off += ATT_W
    for h in range(ATT_HEADS):
        sl = slice(off + h * ATT_HD, off + (h + 1) * ATT_HD)
        k_ref[:, h * ATT_HD:(h + 1) * ATT_HD] = _rope(z[:, sl], ca, sa_lo, sa_hi, ha).astype(k_ref.dtype)
    off += ATT_W
    for j in range(tm // K_CHUNK):
        v_ref[j] = z[j * K_CHUNK:(j + 1) * K_CHUNK, off:off + ATT_W].T.astype(v_ref.dtype)
    off += ATT_W
    for p in range(IDX_HEADS * IDX_HD // LANES):
        sl = slice(off + p * LANES, off + (p + 1) * LANES)
        iq_ref[:, p * LANES:(p + 1) * LANES] = _rope(z[:, sl], ci, si_lo, si_hi, hi).astype(iq_ref.dtype)
    off += IDX_HEADS * IDX_HD
    ik_ref[...] = _rope(z[:, off:off + LANES], ci, si_lo, si_hi, hi).astype(ik_ref.dtype)
    off += LANES
    iw_ref[...] = z[:, off:off + LANES].astype(iw_ref.dtype)
    off += LANES
    mq_ref[...] = z[:, off:off + MEM_W].astype(mq_ref.dtype)


def _dsa_proj(x, g, w, pos, inva, invi):
    t, d = x.shape
    n = w.shape[1]
    tm = TOK_TILE
    dtypes = (BF16, BF16, BF16, BF16, BF16, F32, BF16)
    return pl.pallas_call(
        _dsa_proj_kernel,
        grid=(t // tm,),
        in_specs=[
            pl.BlockSpec((tm, d), lambda i: (i, 0)),
            pl.BlockSpec((1, d), lambda i: (0, 0)),
            pl.BlockSpec((d, n), lambda i: (0, 0)),
            pl.BlockSpec((tm, 1), lambda i: (i, 0)),
            pl.BlockSpec((1, LANES), lambda i: (0, 0)),
            pl.BlockSpec((1, LANES), lambda i: (0, 0)),
        ],
        out_specs=[pl.BlockSpec((tm // K_CHUNK, ATT_W, K_CHUNK), lambda i: (i, 0, 0)) if j == 2
                   else pl.BlockSpec((tm, s), lambda i: (i, 0)) for j, s in enumerate(DSA_SPLITS)],
        out_shape=[jax.ShapeDtypeStruct((t // K_CHUNK, ATT_W, K_CHUNK), dt) if j == 2
                   else jax.ShapeDtypeStruct((t, s), dt) for j, (s, dt) in enumerate(zip(DSA_SPLITS, dtypes))],
        compiler_params=_params(("arbitrary",)),
        name="dsa_proj",
    )(x, g.reshape(1, d), w, pos, inva, invi)


def _lru_kernel(xb_ref, gb_ref, cw_ref, cb_ref, wa_ref, ba_ref, wx_ref, bx_ref, lam_ref, y_ref,
                prev_ref, hc_ref, a_s, b_s):
    ts, c = xb_ref.shape

    @pl.when(pl.program_id(1) == 0)
    def _():
        prev_ref[...] = jnp.zeros_like(prev_ref)
        hc_ref[...] = jnp.zeros_like(hc_ref)

    x = xb_ref[...]
    prev = prev_ref[...]
    xc = cb_ref[...] + cw_ref[3:4, :] * x
    for j in range(1, LRU_CONV):
        xc = xc + cw_ref[LRU_CONV - 1 - j:LRU_CONV - j, :] * _shift_rows(x, prev, j)
    prev_ref[...] = x[ts - SUBLANES:ts, :]

    xcb = xc.astype(BF16)
    ga, gx = [], []
    for t in range(c // MXU_TILE):
        blk = xcb[:, t * MXU_TILE:(t + 1) * MXU_TILE]
        ga.append(jnp.dot(blk, wa_ref[t], preferred_element_type=F32))
        gx.append(jnp.dot(blk, wx_ref[t], preferred_element_type=F32))
    r = jax.nn.sigmoid(jnp.concatenate(ga, axis=1) + ba_ref[...])
    i = jax.nn.sigmoid(jnp.concatenate(gx, axis=1) + bx_ref[...])

    nl = -lam_ref[...]
    softplus = jnp.maximum(nl, 0.0) + jnp.log1p(jnp.exp(-jnp.abs(nl)))
    log_a = (-LRU_C) * r * softplus
    a = jnp.exp(log_a)
    gain = jnp.sqrt(-jnp.tanh(log_a) * (a * a + 1.0))
    bt = gain * (i * xc)

    a3 = a.reshape(ts // SUBLANES, SUBLANES, c)
    b3 = bt.reshape(ts // SUBLANES, SUBLANES, c)
    rid = lax.broadcasted_iota(I32, (1, SUBLANES, c), 1)
    for d in (1, 2, 4):
        a_sh = jnp.where(rid >= d, pltpu.roll(a3, d, 1), 1.0)
        b_sh = jnp.where(rid >= d, pltpu.roll(b3, d, 1), 0.0)
        b3 = a3 * b_sh + b3
        a3 = a3 * a_sh
    a_s[...] = a3.reshape(ts, c)
    b_s[...] = b3.reshape(ts, c)

    def body(g, hc):
        r0 = pl.multiple_of(g * SUBLANES, SUBLANES)
        h = a_s[pl.ds(r0, SUBLANES), :] * hc + b_s[pl.ds(r0, SUBLANES), :]
        b_s[pl.ds(r0, SUBLANES), :] = h
        return jnp.broadcast_to(h[SUBLANES - 1:SUBLANES, :], (SUBLANES, c))

    hc_ref[...] = lax.fori_loop(0, ts // SUBLANES, body, hc_ref[...])
    y_ref[...] = (b_s[...] * jax.nn.gelu(gb_ref[...])).astype(y_ref.dtype)


def _lru_core(xb, gb, cw, cb, wa, ba, wx, bx, lam, batch, seq):
    c = LRU_W
    ts = LRU_TILE
    nt = seq // ts
    row = lambda b, j: (b * nt + j, 0)
    const2 = lambda b, j: (0, 0)
    const3 = lambda b, j: (0, 0, 0)
    return pl.pallas_call(
        _lru_kernel,
        grid=(batch, nt),
        in_specs=[
            pl.BlockSpec((ts, c), row),
            pl.BlockSpec((ts, c), row),
            pl.BlockSpec((LRU_CONV, c), const2),
            pl.BlockSpec((1, c), const2),
            pl.BlockSpec((c // MXU_TILE, MXU_TILE, MXU_TILE), const3),
            pl.BlockSpec((1, c), const2),
            pl.BlockSpec((c // MXU_TILE, MXU_TILE, MXU_TILE), const3),
            pl.BlockSpec((1, c), const2),
            pl.BlockSpec((1, c), const2),
        ],
        out_specs=pl.BlockSpec((ts, c), row),
        out_shape=jax.ShapeDtypeStruct((batch * seq, c), BF16),
        scratch_shapes=[
            pltpu.VMEM((SUBLANES, c), F32),
            pltpu.VMEM((SUBLANES, c), F32),
            pltpu.VMEM((ts, c), F32),
            pltpu.VMEM((ts, c), F32),
        ],
        compiler_params=_params(("arbitrary", "arbitrary")),
        name="lru_core",
    )(xb, gb, cw, cb.reshape(1, c), wa, ba.reshape(1, c), wx, bx.reshape(1, c), lam.reshape(1, c))


def _block_diag_tiles(w):
    per = MXU_TILE // LRU_BW
    w4 = w.reshape(LRU_BLOCKS // per, per, LRU_BW, LRU_BW)
    eye = jnp.eye(per, dtype=w.dtype)
    t = jnp.einsum('gpij,pq->gpiqj', w4, eye)
    return t.reshape(LRU_BLOCKS // per, MXU_TILE, MXU_TILE)


def _mix_out_kernel(a_ref, mq_ref, mk_ref, mv_ref, w_ref, x_ref, o_ref):
    mq = mq_ref[...]
    mk = mk_ref[...]
    mv = mv_ref[...]
    scale = MEM_HD ** -0.5
    heads = []
    for h in range(MEM_HEADS):
        sl = slice(h * MEM_HD, (h + 1) * MEM_HD)
        s = lax.dot_general(mq[:, sl], mk[:, sl], (((1,), (1,)), ((), ())),
                            preferred_element_type=F32) * scale
        s = s - jnp.max(s, axis=-1, keepdims=True)
        e = jnp.exp(s)
        p = e / jnp.sum(e, axis=-1, keepdims=True)
        heads.append(jnp.dot(p.astype(BF16), mv[:, sl], preferred_element_type=F32).astype(BF16))
    cat = jnp.concatenate([a_ref[...]] + heads, axis=1)
    o_ref[...] = x_ref[...] + jnp.dot(cat, w_ref[...], preferred_element_type=F32)


def _mix_out(a, mq, memkv, layer, w_out, x, seq):
    t, d = x.shape
    tm = TOK_TILE
    per = seq // tm
    kin = w_out.shape[0]
    return pl.pallas_call(
        _mix_out_kernel,
        grid=(t // tm,),
        in_specs=[
            pl.BlockSpec((tm, a.shape[1]), lambda i: (i, 0)),
            pl.BlockSpec((tm, MEM_W), lambda i: (i, 0)),
            pl.BlockSpec((MEM_TOKENS, MEM_W), lambda i: (i // per, 2 * layer)),
            pl.BlockSpec((MEM_TOKENS, MEM_W), lambda i: (i // per, 2 * layer + 1)),
            pl.BlockSpec((kin, d), lambda i: (0, 0)),
            pl.BlockSpec((tm, d), lambda i: (i, 0)),
        ],
        out_specs=pl.BlockSpec((tm, d), lambda i: (i, 0)),
        out_shape=jax.ShapeDtypeStruct((t, d), F32),
        compiler_params=_params(("arbitrary",)),
        name="mix_out",
    )(a, mq, memkv, memkv, w_out, x)


def _ffn_kernel(x_ref, g_ref, wup_ref, cw_ref, wdn_ref, fg_ref, o_ref, hn_s, acc_s, uprev_s, *, final_norm):
    tm = x_ref.shape[0]
    nch = wdn_ref.shape[0]

    @pl.when(pl.program_id(1) == 0)
    def _():
        uprev_s[...] = jnp.zeros_like(uprev_s)

    hn_s[...] = _rms(x_ref[...], g_ref[...]).astype(BF16)
    acc_s[...] = jnp.zeros_like(acc_s)

    def up(idx):
        return jnp.dot(hn_s[...], wup_ref[idx], preferred_element_type=F32)

    def conv(u, idx):
        prev = uprev_s[idx]
        cw = cw_ref[idx]
        y = cw[3:4, :] + cw[2:3, :] * u
        y = y + cw[1:2, :] * _shift_rows(u, prev, 1)
        y = y + cw[0:1, :] * _shift_rows(u, prev, 2)
        uprev_s[idx] = u[tm - SUBLANES:tm, :]
        return y

    ug, uv = up(0), up(nch)
    for c in range(nch):
        if c + 1 < nch:
            ug_next, uv_next = up(c + 1), up(c + 1 + nch)
        act = (jax.nn.silu(conv(ug, c)) * conv(uv, c + nch)).astype(BF16)
        acc_s[...] += jnp.dot(act, wdn_ref[c], preferred_element_type=F32)
        if c + 1 < nch:
            ug, uv = ug_next, uv_next
    out = x_ref[...] + acc_s[...]
    if final_norm:
        out = _rms(out, fg_ref[...])
    o_ref[...] = out


def _ffn(x, g, wup3, cw3, wdn3, fg, batch, seq, final_norm):
    t, d = x.shape
    tm = TOK_TILE
    nt = seq // tm
    nch = wdn3.shape[0]
    row = lambda b, j: (b * nt + j, 0)
    return pl.pallas_call(
        functools.partial(_ffn_kernel, final_norm=final_norm),
        grid=(batch, nt),
        in_specs=[
            pl.BlockSpec((tm, d), row),
            pl.BlockSpec((1, d), lambda b, j: (0, 0)),
            pl.BlockSpec(wup3.shape, lambda b, j: (0, 0, 0)),
            pl.BlockSpec(cw3.shape, lambda b, j: (0, 0, 0)),
            pl.BlockSpec(wdn3.shape, lambda b, j: (0, 0, 0)),
            pl.BlockSpec((1, d), lambda b, j: (0, 0)),
        ],
        out_specs=pl.BlockSpec((tm, d), row),
        out_shape=jax.ShapeDtypeStruct((t, d), F32),
        scratch_shapes=[
            pltpu.VMEM((tm, d), BF16),
            pltpu.VMEM((tm, d), F32),
            pltpu.VMEM((2 * nch, SUBLANES, FF_CHUNK), F32),
        ],
        compiler_params=_params(("arbitrary", "arbitrary")),
        name="ffn",
    )(x, g.reshape(1, d), wup3, cw3, wdn3, fg.reshape(1, d))


def _ffn_weights(w_up, conv_w, conv_b, w_down):
    d = w_up.shape[0]
    n2 = w_up.shape[1]
    nch2 = n2 // FF_CHUNK
    wup3 = w_up.astype(BF16).reshape(d, nch2, FF_CHUNK).transpose(1, 0, 2)
    cw = jnp.concatenate([conv_w, conv_b[None, :],
                          jnp.zeros((SUBLANES - FFN_CONV - 1, n2), F32)], axis=0)
    cw3 = cw.reshape(SUBLANES, nch2, FF_CHUNK).transpose(1, 0, 2)
    wdn3 = w_down.astype(BF16).reshape(nch2 // 2, FF_CHUNK, d)
    return wup3, cw3, wdn3


def _dsa_attn_kernel(ik_ref, iq_ref, iw_ref, q_ref, k_ref, vt_ref, o_ref,
                     keys_s, m_s, l_s, acc_s, p_s, alpha_s):
    tq = q_ref.shape[0]
    kc_n = K_CHUNK
    qi = pl.program_id(1)
    nk = qi + 1
    grp = kc_n // SUBLANES

    iw_t = iw_ref[...].T[0:IDX_HEADS, :] * ((IDX_HEADS ** -0.5) * (IDX_HD ** -0.5))
    iq = iq_ref[...]
    lane = lax.broadcasted_iota(I32, (kc_n, LANES), 1)
    krow = lax.broadcasted_iota(I32, (kc_n, tq), 0)
    qcol = lax.broadcasted_iota(I32, (kc_n, tq), 1)

    def score_body(kc, carry):
        r0 = pl.multiple_of(kc * kc_n, kc_n)
        ik2 = ik_ref[pl.ds(r0, kc_n), :]
        ik_lo = jnp.where(lane < IDX_HD, ik2, jnp.zeros_like(ik2))
        ik_hi = jnp.where(lane >= IDX_HD, ik2, jnp.zeros_like(ik2))
        score = jnp.zeros((kc_n, tq), F32)
        for h in range(IDX_HEADS):
            pair = iq[:, (h // 2) * LANES:(h // 2 + 1) * LANES]
            lhs = ik_lo if h % 2 == 0 else ik_hi
            rel = lax.dot_general(lhs, pair, (((1,), (1,)), ((), ())), preferred_element_type=F32)
            score = score + iw_t[h:h + 1, :] * jnp.maximum(rel, 0.0)
        causal = (krow + r0) <= (qcol + qi * tq)
        keys_s[pl.ds(r0, kc_n), :] = jnp.where(causal, score, -jnp.inf)
        return carry

    lax.fori_loop(0, nk, score_body, 0)

    def count(pred_fn):
        def body(kc, part):
            r0 = pl.multiple_of(kc * kc_n, kc_n)
            ones = jnp.where(pred_fn(keys_s[pl.ds(r0, kc_n), :], r0), 1, 0).astype(I32)
            return part + jnp.sum(ones.reshape(grp, SUBLANES, tq), axis=0)
        part = lax.fori_loop(0, nk, body, jnp.zeros((SUBLANES, tq), I32))
        return jnp.sum(part, axis=0, keepdims=True)

    def ordered_to_f32(u):
        neg_inf_u = jnp.int32(0x007FFFFF)
        u = jnp.where((u >= 0) & (u < neg_inf_u), neg_inf_u, u)
        k = u ^ INT_MIN
        return pltpu.bitcast(k ^ ((k >> 31) & jnp.int32(0x7FFFFFFF)), F32)

    def bit_body(t, tu):
        cand_u = tu | lax.shift_left(jnp.int32(1), jnp.int32(31) - t)
        cand_f = ordered_to_f32(cand_u)
        cnt = count(lambda kv, r0: kv >= cand_f)
        return jnp.where(cnt >= TOPK_MAX, cand_u, tu)

    tu = lax.fori_loop(0, 32, bit_body, jnp.zeros((1, tq), I32))
    thr = ordered_to_f32(tu)
    short = thr == -jnp.inf
    cnt_gt = count(lambda kv, r0: kv > thr)
    cnt_ge = count(lambda kv, r0: kv >= thr)
    need = TOPK_MAX - cnt_gt
    excess = jnp.where(short, 0, cnt_ge - TOPK_MAX)

    def tie_search():
        def jbit(t, jc):
            cand = jc | lax.shift_left(jnp.int32(1), jnp.int32(11) - t)
            cnt = count(lambda kv, r0: (kv == thr) & ((krow + r0) < cand))
            return jnp.where(cnt <= need, cand, jc)
        return lax.fori_loop(0, 12, jbit, jnp.zeros((1, tq), I32))

    jcut = lax.cond(jnp.max(excess) > 0, tie_search, lambda: jnp.full((1, tq), 4095, I32))

    def bias_body(kc, carry):
        r0 = pl.multiple_of(kc * kc_n, kc_n)
        kv = keys_s[pl.ds(r0, kc_n), :]
        sel = ((kv > thr) | ((kv == thr) & ((krow + r0) < jcut))) & (kv > -jnp.inf)
        keys_s[pl.ds(r0, kc_n), :] = jnp.where(sel, 0.0, NEG_BIG).astype(F32)
        return carry

    lax.fori_loop(0, nk, bias_body, 0)

    m_s[...] = jnp.full_like(m_s, NEG_BIG)
    l_s[...] = jnp.zeros_like(l_s)
    acc_s[...] = jnp.zeros_like(acc_s)

    def softmax_stage(kc):
        slot = kc & 1
        r0 = pl.multiple_of(kc * kc_n, kc_n)
        bias = keys_s[pl.ds(r0, kc_n), :]
        m_all = m_s[...]
        l_all = l_s[...]
        m_rows, l_rows, a_rows = [], [], []
        for h in range(ATT_HEADS):
            sl = slice(h * ATT_HD, (h + 1) * ATT_HD)
            s = lax.dot_general(k_ref[pl.ds(r0, kc_n), sl], q_ref[:, sl], (((1,), (1,)), ((), ())),
                                preferred_element_type=F32) + bias
            m_prev = m_all[h:h + 1, :]
            m_new = jnp.maximum(m_prev, jnp.max(s, axis=0, keepdims=True))
            alpha = jnp.exp2(m_prev - m_new)
            p = jnp.exp2(s - m_new)
            l_rows.append(alpha * l_all[h:h + 1, :] + jnp.sum(p, axis=0, keepdims=True))
            m_rows.append(m_new)
            a_rows.append(alpha)
            p_s[slot, h * kc_n:(h + 1) * kc_n, :] = p.astype(BF16)
        m_s[...] = jnp.concatenate(m_rows, axis=0)
        l_s[...] = jnp.concatenate(l_rows, axis=0)
        alpha_s[slot] = jnp.concatenate(a_rows, axis=0)

    def pv_stage(kc):
        slot = kc & 1
        al = alpha_s[slot]
        for h in range(ATT_HEADS):
            sl = slice(h * ATT_HD, (h + 1) * ATT_HD)
            pv = jnp.dot(vt_ref[kc, sl, :], p_s[slot, h * kc_n:(h + 1) * kc_n, :],
                         preferred_element_type=F32)
            acc_s[sl, :] = al[h:h + 1, :] * acc_s[sl, :] + pv

    def att_body(kc, carry):
        pv_stage(kc - 1)
        softmax_stage(kc)
        return carry

    softmax_stage(jnp.int32(0))
    lax.fori_loop(1, nk, att_body, 0)
    pv_stage(nk - 1)
    for h in range(ATT_HEADS):
        sl = slice(h * ATT_HD, (h + 1) * ATT_HD)
        o_ref[:, sl] = (acc_s[sl, :] / l_s[h:h + 1, :]).T.astype(o_ref.dtype)


def _dsa_attn(ik2, iq, iw, q, k, vt, batch, seq):
    tq = Q_TILE
    nq = seq // tq
    nkc = seq // K_CHUNK
    r3 = lambda a: a.reshape(batch, seq, a.shape[-1])
    tile = lambda b, j: (b, j, 0)
    full = lambda b, j: (b, 0, 0)
    out = pl.pallas_call(
        _dsa_attn_kernel,
        grid=(batch, nq),
        in_specs=[
            pl.BlockSpec((None, seq, LANES), full),
            pl.BlockSpec((None, tq, IDX_HEADS * IDX_HD), tile),
            pl.BlockSpec((None, tq, LANES), tile),
            pl.BlockSpec((None, tq, ATT_W), tile),
            pl.BlockSpec((None, seq, ATT_W), full),
            pl.BlockSpec((None, nkc, ATT_W, K_CHUNK), lambda b, j: (b, 0, 0, 0)),
        ],
        out_specs=pl.BlockSpec((None, tq, ATT_W), tile),
        out_shape=jax.ShapeDtypeStruct((batch, seq, ATT_W), BF16),
        scratch_shapes=[
            pltpu.VMEM((seq, tq), F32),
            pltpu.VMEM((ATT_HEADS, tq), F32),
            pltpu.VMEM((ATT_HEADS, tq), F32),
            pltpu.VMEM((ATT_W, tq), F32),
            pltpu.VMEM((2, ATT_HEADS * K_CHUNK, tq), BF16),
            pltpu.VMEM((2, ATT_HEADS, tq), F32),
        ],
        compiler_params=_params(("arbitrary", "arbitrary")),
        name="dsa_attn",
    )(r3(ik2), r3(iq), r3(iw), r3(q), r3(k), vt.reshape(batch, nkc, ATT_W, K_CHUNK))
    return out.reshape(batch * seq, ATT_W)


def _rope_inv_rows():
    inv_a = ROPE_THETA ** (-jnp.arange(0, ATT_ROT, 2, dtype=F32) / ATT_ROT)
    inv_i = ROPE_THETA ** (-jnp.arange(0, IDX_ROT, 2, dtype=F32) / IDX_ROT)
    row_a = jnp.concatenate([inv_a, inv_a, jnp.zeros((ATT_HD - ATT_ROT,), F32)])
    half = jnp.concatenate([inv_i, inv_i, jnp.zeros((IDX_HD - IDX_ROT,), F32)])
    row_i = jnp.concatenate([half, half])
    return row_a.reshape(1, LANES), row_i.reshape(1, LANES)


def kernel(x, mem, positions, norm_mix, norm_ffn, mem_norm, final_norm, w_mem_kv, w_ffn_up, ffn_conv_w,
           ffn_conv_b, w_ffn_down, lru_w_in, lru_conv_w, lru_conv_b, lru_w_a, lru_b_a, lru_w_x, lru_b_x,
           lru_lambda, lru_w_out, dsa_w_in, dsa_w_out):
    batch, seq, d = x.shape
    t = batch * seq
    xf = x.reshape(t, d)

    w_kv = jnp.concatenate([w_mem_kv[0], w_mem_kv[1]], axis=1).astype(BF16)
    (memkv,) = _norm_proj(mem.reshape(batch * MEM_TOKENS, d), mem_norm, w_kv,
                          (w_kv.shape[1],), (BF16,), "mem_kv")

    xb, gb, mq = _norm_proj(xf, norm_mix[0], lru_w_in[0].astype(BF16),
                            (LRU_W, LRU_W, MEM_W), (F32, F32, BF16), "lru_proj")
    y = _lru_core(xb, gb, lru_conv_w[0], lru_conv_b[0],
                  _block_diag_tiles(lru_w_a[0]).astype(BF16), lru_b_a[0],
                  _block_diag_tiles(lru_w_x[0]).astype(BF16), lru_b_x[0], lru_lambda[0], batch, seq)
    xf = _mix_out(y, mq, memkv, 0, lru_w_out[0].astype(BF16), xf, seq)
    xf = _ffn(xf, norm_ffn[0], *_ffn_weights(w_ffn_up[0], ffn_conv_w[0], ffn_conv_b[0], w_ffn_down[0]),
              final_norm, batch, seq, False)

    w = dsa_w_in[0]
    o = np.cumsum((0, ATT_W, ATT_W, ATT_W, IDX_HEADS * IDX_HD, IDX_HD, IDX_HEADS, MEM_W))
    w_ik = w[:, o[4]:o[5]]
    w_iw = jnp.pad(w[:, o[5]:o[6]], ((0, 0), (0, LANES - IDX_HEADS)))
    w_cat = jnp.concatenate([w[:, :o[4]], w_ik, w_ik, w_iw, w[:, o[6]:o[7]]], axis=1).astype(BF16)
    inva, invi = _rope_inv_rows()
    pos = positions.astype(F32).reshape(t, 1)
    q, k, vt, iq, ik2, iw, mq = _dsa_proj(xf, norm_mix[1], w_cat, pos, inva, invi)
    att = _dsa_attn(ik2, iq, iw, q, k, vt, batch, seq)
    xf = _mix_out(att, mq, memkv, 1, dsa_w_out[0].astype(BF16), xf, seq)
    xf = _ffn(xf, norm_ffn[1], *_ffn_weights(w_ffn_up[1], ffn_conv_w[1], ffn_conv_b[1], w_ffn_down[1]),
              final_norm, batch, seq, True)
    return xf.reshape(batch, seq, d)
```

```python
import functools

import numpy as np
import jax
import jax.numpy as jnp
from jax import lax
from jax.experimental import pallas as pl
from jax.experimental.pallas import tpu as pltpu

F32 = jnp.float32
BF16 = jnp.bfloat16
I32 = jnp.int32

D_MODEL = 1024
RMS_EPS = 1e-6
ROPE_THETA = 500000.0

LRU_W = 1024
LRU_BLOCKS = 16
LRU_BW = LRU_W // LRU_BLOCKS
LRU_CONV = 4
LRU_C = 8.0

ATT_HEADS = 8
ATT_HD = 128
ATT_W = ATT_HEADS * ATT_HD
ATT_ROT = ATT_HD // 4
IDX_HEADS = 8
IDX_HD = 64
IDX_ROT = IDX_HD // 4
TOPK_MAX = 256

MEM_TOKENS = 256
MEM_HEADS = 4
MEM_HD = 128
MEM_W = MEM_HEADS * MEM_HD

D_FF = 2816
FFN_CONV = 3

SUBLANES = 8
LANES = 128
MXU_TILE = 256
VMEM_LIMIT_BYTES = 56 * 1024 * 1024

INT_MIN = np.int32(-2 ** 31)
NEG_BIG = -1e30
LOG2_E = 1.4426950408889634

TOK_TILE = 512
FFN_TILE = 512
LRU_TILE = 512
Q_TILE = 256
K_CHUNK = 256
FF_CHUNK = 256


def _params(sem):
    return pltpu.CompilerParams(dimension_semantics=sem, vmem_limit_bytes=VMEM_LIMIT_BYTES)


def _rms(x, g):
    ms = jnp.mean(x * x, axis=-1, keepdims=True)
    return x * lax.rsqrt(ms + RMS_EPS) * g


def _shift_rows(x, prev, j):
    r = pltpu.roll(x, j, 0)
    p = pltpu.roll(prev, j, 0)
    rid = lax.broadcasted_iota(I32, (SUBLANES, x.shape[1]), 0)
    top = jnp.where(rid < j, p, r[0:SUBLANES])
    return jnp.concatenate([top, r[SUBLANES:]], axis=0)


def _norm_proj_kernel(x_ref, g_ref, w_ref, *out_refs, splits):
    hn = _rms(x_ref[...], g_ref[...]).astype(BF16)
    z = jnp.dot(hn, w_ref[...], preferred_element_type=F32)
    off = 0
    for o_ref, n in zip(out_refs, splits):
        o_ref[...] = z[:, off:off + n].astype(o_ref.dtype)
        off += n


def _norm_proj(x, g, w, splits, dtypes, name):
    t, d = x.shape
    n = w.shape[1]
    tm = min(TOK_TILE, t)
    return pl.pallas_call(
        functools.partial(_norm_proj_kernel, splits=splits),
        grid=(t // tm,),
        in_specs=[
            pl.BlockSpec((tm, d), lambda i: (i, 0)),
            pl.BlockSpec((1, d), lambda i: (0, 0)),
            pl.BlockSpec((d, n), lambda i: (0, 0)),
        ],
        out_specs=[pl.BlockSpec((tm, s), lambda i: (i, 0)) for s in splits],
        out_shape=[jax.ShapeDtypeStruct((t, s), dt) for s, dt in zip(splits, dtypes)],
        compiler_params=_params(("arbitrary",)),
        name=name,
    )(x, g.reshape(1, d), w)


DSA_SPLITS = (ATT_W, ATT_W, ATT_W, IDX_HEADS * IDX_HD, LANES, LANES, MEM_W)


def _rope(xh, c, s_lo, s_hi, half):
    return xh * c + pltpu.roll(xh, LANES - half, 1) * s_lo + pltpu.roll(xh, half, 1) * s_hi


def _dsa_proj_kernel(x_ref, g_ref, w_ref, pos_ref, inva_ref, invi_ref,
                     q_ref, k_ref, v_ref, iq_ref, ik_ref, iw_ref, mq_ref):
    hn = _rms(x_ref[...], g_ref[...]).astype(BF16)
    z = jnp.dot(hn, w_ref[...], preferred_element_type=F32)
    tm = z.shape[0]
    pos = pos_ref[...]
    lane = lax.broadcasted_iota(I32, (tm, LANES), 1)
    ang_a = pos * inva_ref[...]
    ca, sa = jnp.cos(ang_a), jnp.sin(ang_a)
    ha = ATT_ROT // 2
    sa_lo = jnp.where(lane < ha, -sa, 0.0)
    sa_hi = jnp.where((lane >= ha) & (lane < 2 * ha), sa, 0.0)
    ang_i = pos * invi_ref[...]
    ci, si = jnp.cos(ang_i), jnp.sin(ang_i)
    hi = IDX_ROT // 2
    m64 = lane & (IDX_HD - 1)
    si_lo = jnp.where(m64 < hi, -si, 0.0)
    si_hi = jnp.where((m64 >= hi) & (m64 < 2 * hi), si, 0.0)

    off = 0
    for h in range(ATT_HEADS):
        sl = slice(off + h * ATT_HD, off + (h + 1) * ATT_HD)
        qh = _rope(z[:, sl], ca, sa_lo, sa_hi, ha) * (ATT_HD ** -0.5 * LOG2_E)
        q_ref[:, h * ATT_HD:(h + 1) * ATT_HD] = qh.astype(q_ref.dtype)
    off += ATT_W
    for h in range(ATT_HEADS):
        sl = slice(off + h * ATT_HD, off + (h + 1) * ATT_HD)
        k_ref[:, h * ATT_HD:(h + 1) * ATT_HD] = _rope(z[:, sl], ca, sa_lo, sa_hi, ha).astype(k_ref.dtype)
    off += ATT_W
    for j in range(tm // K_CHUNK):
        v_ref[j] = z[j * K_CHUNK:(j + 1) * K_CHUNK, off:off + ATT_W].T.astype(v_ref.dtype)
    off += ATT_W
    for p in range(IDX_HEADS * IDX_HD // LANES):
        sl = slice(off + p * LANES, off + (p + 1) * LANES)
        iq_ref[:, p * LANES:(p + 1) * LANES] = _rope(z[:, sl], ci, si_lo, si_hi, hi).astype(iq_ref.dtype)
    off += IDX_HEADS * IDX_HD
    ik_ref[...] = _rope(z[:, off:off + LANES], ci, si_lo, si_hi, hi).astype(ik_ref.dtype)
    off += LANES
    iw_ref[...] = z[:, off:off + LANES].astype(iw_ref.dtype)
    off += LANES
    mq_ref[...] = z[:, off:off + MEM_W].astype(mq_ref.dtype)


def _dsa_proj(x, g, w, pos, inva, invi):
    t, d = x.shape
    n = w.shape[1]
    tm = TOK_TILE
    dtypes = (BF16, BF16, BF16, BF16, BF16, F32, BF16)
    return pl.pallas_call(
        _dsa_proj_kernel,
        grid=(t // tm,),
        in_specs=[
            pl.BlockSpec((tm, d), lambda i: (i, 0)),
            pl.BlockSpec((1, d), lambda i: (0, 0)),
            pl.BlockSpec((d, n), lambda i: (0, 0)),
            pl.BlockSpec((tm, 1), lambda i: (i, 0)),
            pl.BlockSpec((1, LANES), lambda i: (0, 0)),
            pl.BlockSpec((1, LANES), lambda i: (0, 0)),
        ],
        out_specs=[pl.BlockSpec((tm // K_CHUNK, ATT_W, K_CHUNK), lambda i: (i, 0, 0)) if j == 2
                   else pl.BlockSpec((tm, s), lambda i: (i, 0)) for j, s in enumerate(DSA_SPLITS)],
        out_shape=[jax.ShapeDtypeStruct((t // K_CHUNK, ATT_W, K_CHUNK), dt) if j == 2
                   else jax.ShapeDtypeStruct((t, s), dt) for j, (s, dt) in enumerate(zip(DSA_SPLITS, dtypes))],
        compiler_params=_params(("arbitrary",)),
        name="dsa_proj",
    )(x, g.reshape(1, d), w, pos, inva, invi)


def _lru_kernel(xb_ref, gb_ref, cw_ref, cb_ref, wa_ref, ba_ref, wx_ref, bx_ref, lam_ref, y_ref,
                prev_ref, hc_ref, a_s, b_s):
    ts, c = xb_ref.shape

    @pl.when(pl.program_id(1) == 0)
    def _():
        prev_ref[...] = jnp.zeros_like(prev_ref)
        hc_ref[...] = jnp.zeros_like(hc_ref)

    x = xb_ref[...]
    prev = prev_ref[...]
    xc = cb_ref[...] + cw_ref[3:4, :] * x
    for j in range(1, LRU_CONV):
        xc = xc + cw_ref[LRU_CONV - 1 - j:LRU_CONV - j, :] * _shift_rows(x, prev, j)
    prev_ref[...] = x[ts - SUBLANES:ts, :]

    xcb = xc.astype(BF16)
    ga, gx = [], []
    for t in range(c // MXU_TILE):
        blk = xcb[:, t * MXU_TILE:(t + 1) * MXU_TILE]
        ga.append(jnp.dot(blk, wa_ref[t], preferred_element_type=F32))
        gx.append(jnp.dot(blk, wx_ref[t], preferred_element_type=F32))
    r = jax.nn.sigmoid(jnp.concatenate(ga, axis=1) + ba_ref[...])
    i = jax.nn.sigmoid(jnp.concatenate(gx, axis=1) + bx_ref[...])

    nl = -lam_ref[...]
    softplus = jnp.maximum(nl, 0.0) + jnp.log1p(jnp.exp(-jnp.abs(nl)))
    log_a = (-LRU_C) * r * softplus
    a = jnp.exp(log_a)
    gain = jnp.sqrt(-jnp.tanh(log_a) * (a * a + 1.0))
    bt = gain * (i * xc)

    a3 = a.reshape(ts // SUBLANES, SUBLANES, c)
    b3 = bt.reshape(ts // SUBLANES, SUBLANES, c)
    rid = lax.broadcasted_iota(I32, (1, SUBLANES, c), 1)
    for d in (1, 2, 4):
        a_sh = jnp.where(rid >= d, pltpu.roll(a3, d, 1), 1.0)
        b_sh = jnp.where(rid >= d, pltpu.roll(b3, d, 1), 0.0)
        b3 = a3 * b_sh + b3
        a3 = a3 * a_sh
    a_s[...] = a3.reshape(ts, c)
    b_s[...] = b3.reshape(ts, c)

    def body(g, hc):
        r0 = pl.multiple_of(g * SUBLANES, SUBLANES)
        h = a_s[pl.ds(r0, SUBLANES), :] * hc + b_s[pl.ds(r0, SUBLANES), :]
        b_s[pl.ds(r0, SUBLANES), :] = h
        return jnp.broadcast_to(h[SUBLANES - 1:SUBLANES, :], (SUBLANES, c))

    hc_ref[...] = lax.fori_loop(0, ts // SUBLANES, body, hc_ref[...])
    y_ref[...] = (b_s[...] * jax.nn.gelu(gb_ref[...])).astype(y_ref.dtype)


def _lru_core(xb, gb, cw, cb, wa, ba, wx, bx, lam, batch, seq):
    c = LRU_W
    ts = LRU_TILE
    nt = seq // ts
    row = lambda b, j: (b * nt + j, 0)
    const2 = lambda b, j: (0, 0)
    const3 = lambda b, j: (0, 0, 0)
    return pl.pallas_call(
        _lru_kernel,
        grid=(batch, nt),
        in_specs=[
            pl.BlockSpec((ts, c), row),
            pl.BlockSpec((ts, c), row),
            pl.BlockSpec((LRU_CONV, c), const2),
            pl.BlockSpec((1, c), const2),
            pl.BlockSpec((c // MXU_TILE, MXU_TILE, MXU_TILE), const3),
            pl.BlockSpec((1, c), const2),
            pl.BlockSpec((c // MXU_TILE, MXU_TILE, MXU_TILE), const3),
            pl.BlockSpec((1, c), const2),
            pl.BlockSpec((1, c), const2),
        ],
        out_specs=pl.BlockSpec((ts, c), row),
        out_shape=jax.ShapeDtypeStruct((batch * seq, c), BF16),
        scratch_shapes=[
            pltpu.VMEM((SUBLANES, c), F32),
            pltpu.VMEM((SUBLANES, c), F32),
            pltpu.VMEM((ts, c), F32),
            pltpu.VMEM((ts, c), F32),
        ],
        compiler_params=_params(("arbitrary", "arbitrary")),
        name="lru_core",
    )(xb, gb, cw, cb.reshape(1, c), wa, ba.reshape(1, c), wx, bx.reshape(1, c), lam.reshape(1, c))


def _block_diag_tiles(w):
    per = MXU_TILE // LRU_BW
    w4 = w.reshape(LRU_BLOCKS // per, per, LRU_BW, LRU_BW)
    eye = jnp.eye(per, dtype=w.dtype)
    t = jnp.einsum('gpij,pq->gpiqj', w4, eye)
    return t.reshape(LRU_BLOCKS // per, MXU_TILE, MXU_TILE)


def _mix_out_kernel(a_ref, mq_ref, mk_ref, mv_ref, w_ref, x_ref, o_ref):
    mq = mq_ref[...]
    mk = mk_ref[...]
    mv = mv_ref[...]
    scale = MEM_HD ** -0.5
    heads = []
    for h in range(MEM_HEADS):
        sl = slice(h * MEM_HD, (h + 1) * MEM_HD)
        s = lax.dot_general(mq[:, sl], mk[:, sl], (((1,), (1,)), ((), ())),
                            preferred_element_type=F32) * scale
        s = s - jnp.max(s, axis=-1, keepdims=True)
        e = jnp.exp(s)
        p = e / jnp.sum(e, axis=-1, keepdims=True)
        heads.append(jnp.dot(p.astype(BF16), mv[:, sl], preferred_element_type=F32).astype(BF16))
    cat = jnp.concatenate([a_ref[...]] + heads, axis=1)
    o_ref[...] = x_ref[...] + jnp.dot(cat, w_ref[...], preferred_element_type=F32)


def _mix_out(a, mq, memkv, layer, w_out, x, seq):
    t, d = x.shape
    tm = TOK_TILE
    per = seq // tm
    kin = w_out.shape[0]
    return pl.pallas_call(
        _mix_out_kernel,
        grid=(t // tm,),
        in_specs=[
            pl.BlockSpec((tm, a.shape[1]), lambda i: (i, 0)),
            pl.BlockSpec((tm, MEM_W), lambda i: (i, 0)),
            pl.BlockSpec((MEM_TOKENS, MEM_W), lambda i: (i // per, 2 * layer)),
            pl.BlockSpec((MEM_TOKENS, MEM_W), lambda i: (i // per, 2 * layer + 1)),
            pl.BlockSpec((kin, d), lambda i: (0, 0)),
            pl.BlockSpec((tm, d), lambda i: (i, 0)),
        ],
        out_specs=pl.BlockSpec((tm, d), lambda i: (i, 0)),
        out_shape=jax.ShapeDtypeStruct((t, d), F32),
        compiler_params=_params(("arbitrary",)),
        name="mix_out",
    )(a, mq, memkv, memkv, w_out, x)


def _ffn_kernel(x_ref, g_ref, wup_ref, cw_ref, cb_ref, wdn_ref, fg_ref, o_ref, hn_s, acc_s, uprev_s,
                *, final_norm):
    tm = x_ref.shape[0]
    nch = wdn_ref.shape[0] // FF_CHUNK

    @pl.when(pl.program_id(1) == 0)
    def _():
        uprev_s[...] = jnp.zeros_like(uprev_s)

    hn_s[...] = _rms(x_ref[...], g_ref[...]).astype(BF16)
    acc_s[...] = jnp.zeros_like(acc_s)

    def cols(idx):
        return slice(idx * FF_CHUNK, (idx + 1) * FF_CHUNK)

    def up(idx):
        return jnp.dot(hn_s[...], wup_ref[:, cols(idx)], preferred_element_type=F32)

    def conv(u, idx):
        prev = uprev_s[idx]
        y = cb_ref[:, cols(idx)] + cw_ref[2:3, cols(idx)] * u
        y = y + cw_ref[1:2, cols(idx)] * _shift_rows(u, prev, 1)
        y = y + cw_ref[0:1, cols(idx)] * _shift_rows(u, prev, 2)
        uprev_s[idx] = u[tm - SUBLANES:tm, :]
        return y

    ug, uv = up(0), up(nch)
    for c in range(nch):
        if c + 1 < nch:
            ug_next, uv_next = up(c + 1), up(c + 1 + nch)
        act = (jax.nn.silu(conv(ug, c)) * conv(uv, c + nch)).astype(BF16)
        acc_s[...] += jnp.dot(act, wdn_ref[c * FF_CHUNK:(c + 1) * FF_CHUNK, :], preferred_element_type=F32)
        if c + 1 < nch:
            ug, uv = ug_next, uv_next
    out = x_ref[...] + acc_s[...]
    if final_norm:
        out = _rms(out, fg_ref[...])
    o_ref[...] = out


def _ffn(x, g, w_up, conv_w, conv_b, w_down, fg, batch, seq, final_norm):
    t, d = x.shape
    tm = FFN_TILE
    nt = seq // tm
    n2 = w_up.shape[1]
    row = lambda b, j: (b * nt + j, 0)
    const = lambda b, j: (0, 0)
    return pl.pallas_call(
        functools.partial(_ffn_kernel, final_norm=final_norm),
        grid=(batch, nt),
        in_specs=[
            pl.BlockSpec((tm, d), row),
            pl.BlockSpec((1, d), const),
            pl.BlockSpec(w_up.shape, const),
            pl.BlockSpec(conv_w.shape, const),
            pl.BlockSpec((1, n2), const),
            pl.BlockSpec(w_down.shape, const),
            pl.BlockSpec((1, d), const),
        ],
        out_specs=pl.BlockSpec((tm, d), row),
        out_shape=jax.ShapeDtypeStruct((t, d), F32),
        scratch_shapes=[
            pltpu.VMEM((tm, d), BF16),
            pltpu.VMEM((tm, d), F32),
            pltpu.VMEM((n2 // FF_CHUNK, SUBLANES, FF_CHUNK), F32),
        ],
        compiler_params=_params(("arbitrary", "arbitrary")),
        name="ffn",
    )(x, g.reshape(1, d), w_up.astype(BF16), conv_w, conv_b.reshape(1, n2), w_down.astype(BF16),
      fg.reshape(1, d))


def _dsa_attn_kernel(ik_ref, iq_ref, iw_ref, q_ref, k_ref, vt_ref, o_ref,
                     keys_s, m_s, l_s, acc_s, p_s, alpha_s):
    tq = q_ref.shape[0]
    kc_n = K_CHUNK
    qi = pl.program_id(1)
    nk = qi + 1
    grp = kc_n // SUBLANES

    iw_t = iw_ref[...].T[0:IDX_HEADS, :] * ((IDX_HEADS ** -0.5) * (IDX_HD ** -0.5))
    iq = iq_ref[...]
    lane = lax.broadcasted_iota(I32, (kc_n, LANES), 1)
    krow = lax.broadcasted_iota(I32, (kc_n, tq), 0)
    qcol = lax.broadcasted_iota(I32, (kc_n, tq), 1)

    def score_body(kc, carry):
        r0 = pl.multiple_of(kc * kc_n, kc_n)
        ik2 = ik_ref[pl.ds(r0, kc_n), :]
        ik_lo = jnp.where(lane < IDX_HD, ik2, jnp.zeros_like(ik2))
        ik_hi = jnp.where(lane >= IDX_HD, ik2, jnp.zeros_like(ik2))
        score = jnp.zeros((kc_n, tq), F32)
        for h in range(IDX_HEADS):
            pair = iq[:, (h // 2) * LANES:(h // 2 + 1) * LANES]
            lhs = ik_lo if h % 2 == 0 else ik_hi
            rel = lax.dot_general(lhs, pair, (((1,), (1,)), ((), ())), preferred_element_type=F32)
            score = score + iw_t[h:h + 1, :] * jnp.maximum(rel, 0.0)
        causal = (krow + r0) <= (qcol + qi * tq)
        keys_s[pl.ds(r0, kc_n), :] = jnp.where(causal, score, -jnp.inf)
        return carry

    lax.fori_loop(0, nk, score_body, 0)

    def count(pred_fn):
        def body(kc, part):
            r0 = pl.multiple_of(kc * kc_n, kc_n)
            ones = jnp.where(pred_fn(keys_s[pl.ds(r0, kc_n), :], r0), 1, 0).astype(I32)
            return part + jnp.sum(ones.reshape(grp, SUBLANES, tq), axis=0)
        part = lax.fori_loop(0, nk, body, jnp.zeros((SUBLANES, tq), I32))
        return jnp.sum(part, axis=0, keepdims=True)

    def ordered_to_f32(u):
        neg_inf_u = jnp.int32(0x007FFFFF)
        u = jnp.where((u >= 0) & (u < neg_inf_u), neg_inf_u, u)
        k = u ^ INT_MIN
        return pltpu.bitcast(k ^ ((k >> 31) & jnp.int32(0x7FFFFFFF)), F32)

    def bit_body(t, carry):
        tu, cnt_ge = carry
        cand_u = tu | lax.shift_left(jnp.int32(1), jnp.int32(31) - t)
        cand_f = ordered_to_f32(cand_u)
        cnt = count(lambda kv, r0: kv >= cand_f)
        ok = cnt >= TOPK_MAX
        return jnp.where(ok, cand_u, tu), jnp.where(ok, cnt, cnt_ge)

    zeros = jnp.zeros((1, tq), I32)
    tu, cnt_ge = lax.fori_loop(0, 32, bit_body, (zeros, zeros + nk * kc_n))
    thr = ordered_to_f32(tu)
    short = thr == -jnp.inf
    excess = jnp.where(short, 0, cnt_ge - TOPK_MAX)

    def tie_search():
        need = TOPK_MAX - count(lambda kv, r0: kv > thr)

        def jbit(t, jc):
            cand = jc | lax.shift_left(jnp.int32(1), jnp.int32(11) - t)
            cnt = count(lambda kv, r0: (kv == thr) & ((krow + r0) < cand))
            return jnp.where(cnt <= need, cand, jc)
        return lax.fori_loop(0, 12, jbit, jnp.zeros((1, tq), I32))

    jcut = lax.cond(jnp.max(excess) > 0, tie_search, lambda: jnp.full((1, tq), 4095, I32))

    def bias_body(kc, carry):
        r0 = pl.multiple_of(kc * kc_n, kc_n)
        kv = keys_s[pl.ds(r0, kc_n), :]
        sel = ((kv > thr) | ((kv == thr) & ((krow + r0) < jcut))) & (kv > -jnp.inf)
        keys_s[pl.ds(r0, kc_n), :] = jnp.where(sel, 0.0, NEG_BIG).astype(F32)
        return carry

    lax.fori_loop(0, nk, bias_body, 0)

    m_s[...] = jnp.full_like(m_s, NEG_BIG)
    l_s[...] = jnp.zeros_like(l_s)
    acc_s[...] = jnp.zeros_like(acc_s)

    def softmax_stage(kc):
        slot = kc & 1
        r0 = pl.multiple_of(kc * kc_n, kc_n)
        bias = keys_s[pl.ds(r0, kc_n), :]
        m_all = m_s[...]
        l_all = l_s[...]
        m_rows, l_rows, a_rows = [], [], []
        for h in range(ATT_HEADS):
            sl = slice(h * ATT_HD, (h + 1) * ATT_HD)
            s = lax.dot_general(k_ref[pl.ds(r0, kc_n), sl], q_ref[:, sl], (((1,), (1,)), ((), ())),
                                preferred_element_type=F32) + bias
            m_prev = m_all[h:h + 1, :]
            m_new = jnp.maximum(m_prev, jnp.max(s, axis=0, keepdims=True))
            alpha = jnp.exp2(m_prev - m_new)
            p = jnp.exp2(s - m_new)
            l_rows.append(alpha * l_all[h:h + 1, :] + jnp.sum(p, axis=0, keepdims=True))
            m_rows.append(m_new)
            a_rows.append(alpha)
            p_s[slot, h * kc_n:(h + 1) * kc_n, :] = p.astype(BF16)
        m_s[...] = jnp.concatenate(m_rows, axis=0)
        l_s[...] = jnp.concatenate(l_rows, axis=0)
        alpha_s[slot] = jnp.concatenate(a_rows, axis=0)

    def pv_stage(kc):
        slot = kc & 1
        al = alpha_s[slot]
        for h in range(ATT_HEADS):
            sl = slice(h * ATT_HD, (h + 1) * ATT_HD)
            pv = jnp.dot(vt_ref[kc, sl, :], p_s[slot, h * kc_n:(h + 1) * kc_n, :],
                         preferred_element_type=F32)
            acc_s[sl, :] = al[h:h + 1, :] * acc_s[sl, :] + pv

    def att_body(kc, carry):
        pv_stage(kc - 1)
        softmax_stage(kc)
        return carry

    softmax_stage(jnp.int32(0))
    lax.fori_loop(1, nk, att_body, 0)
    pv_stage(nk - 1)
    for h in range(ATT_HEADS):
        sl = slice(h * ATT_HD, (h + 1) * ATT_HD)
        o_ref[:, sl] = (acc_s[sl, :] / l_s[h:h + 1, :]).T.astype(o_ref.dtype)


def _dsa_attn(ik2, iq, iw, q, k, vt, batch, seq):
    tq = Q_TILE
    nq = seq // tq
    nkc = seq // K_CHUNK
    r3 = lambda a: a.reshape(batch, seq, a.shape[-1])
    tile = lambda b, j: (b, j, 0)
    full = lambda b, j: (b, 0, 0)
    out = pl.pallas_call(
        _dsa_attn_kernel,
        grid=(batch, nq),
        in_specs=[
            pl.BlockSpec((None, seq, LANES), full),
            pl.BlockSpec((None, tq, IDX_HEADS * IDX_HD), tile),
            pl.BlockSpec((None, tq, LANES), tile),
            pl.BlockSpec((None, tq, ATT_W), tile),
            pl.BlockSpec((None, seq, ATT_W), full),
            pl.BlockSpec((None, nkc, ATT_W, K_CHUNK), lambda b, j: (b, 0, 0, 0)),
        ],
        out_specs=pl.BlockSpec((None, tq, ATT_W), tile),
        out_shape=jax.ShapeDtypeStruct((batch, seq, ATT_W), BF16),
        scratch_shapes=[
            pltpu.VMEM((seq, tq), F32),
            pltpu.VMEM((ATT_HEADS, tq), F32),
            pltpu.VMEM((ATT_HEADS, tq), F32),
            pltpu.VMEM((ATT_W, tq), F32),
            pltpu.VMEM((2, ATT_HEADS * K_CHUNK, tq), BF16),
            pltpu.VMEM((2, ATT_HEADS, tq), F32),
        ],
        compiler_params=_params(("arbitrary", "arbitrary")),
        name="dsa_attn",
    )(r3(ik2), r3(iq), r3(iw), r3(q), r3(k), vt.reshape(batch, nkc, ATT_W, K_CHUNK))
    return out.reshape(batch * seq, ATT_W)


def _rope_inv_rows():
    inv_a = ROPE_THETA ** (-jnp.arange(0, ATT_ROT, 2, dtype=F32) / ATT_ROT)
    inv_i = ROPE_THETA ** (-jnp.arange(0, IDX_ROT, 2, dtype=F32) / IDX_ROT)
    row_a = jnp.concatenate([inv_a, inv_a, jnp.zeros((ATT_HD - ATT_ROT,), F32)])
    half = jnp.concatenate([inv_i, inv_i, jnp.zeros((IDX_HD - IDX_ROT,), F32)])
    row_i = jnp.concatenate([half, half])
    return row_a.reshape(1, LANES), row_i.reshape(1, LANES)


def kernel(x, mem, positions, norm_mix, norm_ffn, mem_norm, final_norm, w_mem_kv, w_ffn_up, ffn_conv_w,
           ffn_conv_b, w_ffn_down, lru_w_in, lru_conv_w, lru_conv_b, lru_w_a, lru_b_a, lru_w_x, lru_b_x,
           lru_lambda, lru_w_out, dsa_w_in, dsa_w_out):
    batch, seq, d = x.shape
    t = batch * seq
    xf = x.reshape(t, d)

    w_kv = jnp.concatenate([w_mem_kv[0], w_mem_kv[1]], axis=1).astype(BF16)
    (memkv,) = _norm_proj(mem.reshape(batch * MEM_TOKENS, d), mem_norm, w_kv,
                          (w_kv.shape[1],), (BF16,), "mem_kv")

    xb, gb, mq = _norm_proj(xf, norm_mix[0], lru_w_in[0].astype(BF16),
                            (LRU_W, LRU_W, MEM_W), (F32, F32, BF16), "lru_proj")
    y = _lru_core(xb, gb, lru_conv_w[0], lru_conv_b[0],
                  _block_diag_tiles(lru_w_a[0]).astype(BF16), lru_b_a[0],
                  _block_diag_tiles(lru_w_x[0]).astype(BF16), lru_b_x[0], lru_lambda[0], batch, seq)
    xf = _mix_out(y, mq, memkv, 0, lru_w_out[0].astype(BF16), xf, seq)
    xf = _ffn(xf, norm_ffn[0], w_ffn_up[0], ffn_conv_w[0], ffn_conv_b[0], w_ffn_down[0],
              final_norm, batch, seq, False)

    w = dsa_w_in[0].astype(BF16)
    o = np.cumsum((0, ATT_W, ATT_W, ATT_W, IDX_HEADS * IDX_HD, IDX_HD, IDX_HEADS, MEM_W))
    w_ik = w[:, o[4]:o[5]]
    w_iw = jnp.pad(w[:, o[5]:o[6]], ((0, 0), (0, LANES - IDX_HEADS)))
    w_cat = jnp.concatenate([w[:, :o[4]], w_ik, w_ik, w_iw, w[:, o[6]:o[7]]], axis=1)
    inva, invi = _rope_inv_rows()
    pos = positions.astype(F32).reshape(t, 1)
    q, k, vt, iq, ik2, iw, mq = _dsa_proj(xf, norm_mix[1], w_cat, pos, inva, invi)
    att = _dsa_attn(ik2, iq, iw, q, k, vt, batch, seq)
    xf = _mix_out(att, mq, memkv, 1, dsa_w_out[0].astype(BF16), xf, seq)
    xf = _ffn(xf, norm_ffn[1], w_ffn_up[1], ffn_conv_w[1], ffn_conv_b[1], w_ffn_down[1],
              final_norm, batch, seq, True)
    return xf.reshape(batch, seq, d)
```

```python
import functools

import numpy as np
import jax
import jax.numpy as jnp
from jax import lax
from jax.experimental import pallas as pl
from jax.experimental.pallas import tpu as pltpu

F32 = jnp.float32
BF16 = jnp.bfloat16
I32 = jnp.int32

D_MODEL = 1024
RMS_EPS = 1e-6
ROPE_THETA = 500000.0

LRU_W = 1024
LRU_BLOCKS = 16
LRU_BW = LRU_W // LRU_BLOCKS
LRU_CONV = 4
LRU_C = 8.0

ATT_HEADS = 8
ATT_HD = 128
ATT_W = ATT_HEADS * ATT_HD
ATT_ROT = ATT_HD // 4
IDX_HEADS = 8
IDX_HD = 64
IDX_ROT = IDX_HD // 4
TOPK_MAX = 256

MEM_TOKENS = 256
MEM_HEADS = 4
MEM_HD = 128
MEM_W = MEM_HEADS * MEM_HD

D_FF = 2816
FFN_CONV = 3

SUBLANES = 8
LANES = 128
MXU_TILE = 256
VMEM_LIMIT_BYTES = 56 * 1024 * 1024

INT_MIN = np.int32(-2 ** 31)
NEG_BIG = -1e30
LOG2_E = 1.4426950408889634

TOK_TILE = 512
FFN_TILE = 512
LRU_TILE = 512
Q_TILE = 256
K_CHUNK = 256
FF_CHUNK = 256
COUNT_ACC_ROWS = 32


def _params(sem):
    return pltpu.CompilerParams(dimension_semantics=sem, vmem_limit_bytes=VMEM_LIMIT_BYTES)


def _rms(x, g):
    ms = jnp.mean(x * x, axis=-1, keepdims=True)
    return x * lax.rsqrt(ms + RMS_EPS) * g


def _shift_rows(x, prev, j):
    r = pltpu.roll(x, j, 0)
    p = pltpu.roll(prev, j, 0)
    rid = lax.broadcasted_iota(I32, (SUBLANES, x.shape[1]), 0)
    top = jnp.where(rid < j, p, r[0:SUBLANES])
    return jnp.concatenate([top, r[SUBLANES:]], axis=0)


def _norm_proj_kernel(x_ref, g_ref, w_ref, *out_refs, splits):
    hn = _rms(x_ref[...], g_ref[...]).astype(BF16)
    z = jnp.dot(hn, w_ref[...], preferred_element_type=F32)
    off = 0
    for o_ref, n in zip(out_refs, splits):
        o_ref[...] = z[:, off:off + n].astype(o_ref.dtype)
        off += n


def _norm_proj(x, g, w, splits, dtypes, name):
    t, d = x.shape
    n = w.shape[1]
    tm = min(TOK_TILE, t)
    return pl.pallas_call(
        functools.partial(_norm_proj_kernel, splits=splits),
        grid=(t // tm,),
        in_specs=[
            pl.BlockSpec((tm, d), lambda i: (i, 0)),
            pl.BlockSpec((1, d), lambda i: (0, 0)),
            pl.BlockSpec((d, n), lambda i: (0, 0)),
        ],
        out_specs=[pl.BlockSpec((tm, s), lambda i: (i, 0)) for s in splits],
        out_shape=[jax.ShapeDtypeStruct((t, s), dt) for s, dt in zip(splits, dtypes)],
        compiler_params=_params(("arbitrary",)),
        name=name,
    )(x, g.reshape(1, d), w)


DSA_SPLITS = (ATT_W, ATT_W, ATT_W, IDX_HEADS * IDX_HD, LANES, LANES, MEM_W)


def _rope(xh, c, s_lo, s_hi, half):
    return xh * c + pltpu.roll(xh, LANES - half, 1) * s_lo + pltpu.roll(xh, half, 1) * s_hi


def _dsa_proj_kernel(x_ref, g_ref, w_ref, pos_ref, inva_ref, invi_ref,
                     q_ref, k_ref, v_ref, iq_ref, ik_ref, iw_ref, mq_ref):
    hn = _rms(x_ref[...], g_ref[...]).astype(BF16)
    z = jnp.dot(hn, w_ref[...], preferred_element_type=F32)
    tm = z.shape[0]
    pos = pos_ref[...]
    lane = lax.broadcasted_iota(I32, (tm, LANES), 1)
    ang_a = pos * inva_ref[...]
    ca, sa = jnp.cos(ang_a), jnp.sin(ang_a)
    ha = ATT_ROT // 2
    sa_lo = jnp.where(lane < ha, -sa, 0.0)
    sa_hi = jnp.where((lane >= ha) & (lane < 2 * ha), sa, 0.0)
    ang_i = pos * invi_ref[...]
    ci, si = jnp.cos(ang_i), jnp.sin(ang_i)
    hi = IDX_ROT // 2
    m64 = lane & (IDX_HD - 1)
    si_lo = jnp.where(m64 < hi, -si, 0.0)
    si_hi = jnp.where((m64 >= hi) & (m64 < 2 * hi), si, 0.0)

    off = 0
    for h in range(ATT_HEADS):
        sl = slice(off + h * ATT_HD, off + (h + 1) * ATT_HD)
        qh = _rope(z[:, sl], ca, sa_lo, sa_hi, ha) * (ATT_HD ** -0.5 * LOG2_E)
        q_ref[:, h * ATT_HD:(h + 1) * ATT_HD] = qh.astype(q_ref.dtype)
    off += ATT_W
    for h in range(ATT_HEADS):
        sl = slice(off + h * ATT_HD, off + (h + 1) * ATT_HD)
        k_ref[:, h * ATT_HD:(h + 1) * ATT_HD] = _rope(z[:, sl], ca, sa_lo, sa_hi, ha).astype(k_ref.dtype)
    off += ATT_W
    for j in range(tm // K_CHUNK):
        v_ref[j] = z[j * K_CHUNK:(j + 1) * K_CHUNK, off:off + ATT_W].T.astype(v_ref.dtype)
    off += ATT_W
    for p in range(IDX_HEADS * IDX_HD // LANES):
        sl = slice(off + p * LANES, off + (p + 1) * LANES)
        iq_ref[:, p * LANES:(p + 1) * LANES] = _rope(z[:, sl], ci, si_lo, si_hi, hi).astype(iq_ref.dtype)
    off += IDX_HEADS * IDX_HD
    ik_ref[...] = _rope(z[:, off:off + LANES], ci, si_lo, si_hi, hi).astype(ik_ref.dtype)
    off += LANES
    iw_ref[...] = z[:, off:off + LANES].astype(iw_ref.dtype)
    off += LANES
    mq_ref[...] = z[:, off:off + MEM_W].astype(mq_ref.dtype)


def _dsa_proj(x, g, w, pos, inva, invi):
    t, d = x.shape
    n = w.shape[1]
    tm = TOK_TILE
    dtypes = (BF16, BF16, BF16, BF16, BF16, F32, BF16)
    return pl.pallas_call(
        _dsa_proj_kernel,
        grid=(t // tm,),
        in_specs=[
            pl.BlockSpec((tm, d), lambda i: (i, 0)),
            pl.BlockSpec((1, d), lambda i: (0, 0)),
            pl.BlockSpec((d, n), lambda i: (0, 0)),
            pl.BlockSpec((tm, 1), lambda i: (i, 0)),
            pl.BlockSpec((1, LANES), lambda i: (0, 0)),
            pl.BlockSpec((1, LANES), lambda i: (0, 0)),
        ],
        out_specs=[pl.BlockSpec((tm // K_CHUNK, ATT_W, K_CHUNK), lambda i: (i, 0, 0)) if j == 2
                   else pl.BlockSpec((tm, s), lambda i: (i, 0)) for j, s in enumerate(DSA_SPLITS)],
        out_shape=[jax.ShapeDtypeStruct((t // K_CHUNK, ATT_W, K_CHUNK), dt) if j == 2
                   else jax.ShapeDtypeStruct((t, s), dt) for j, (s, dt) in enumerate(zip(DSA_SPLITS, dtypes))],
        compiler_params=_params(("arbitrary",)),
        name="dsa_proj",
    )(x, g.reshape(1, d), w, pos, inva, invi)


def _lru_kernel(xb_ref, gb_ref, cw_ref, cb_ref, wa_ref, ba_ref, wx_ref, bx_ref, lam_ref, y_ref,
                prev_ref, hc_ref, a_s, b_s):
    ts, c = xb_ref.shape

    @pl.when(pl.program_id(1) == 0)
    def _():
        prev_ref[...] = jnp.zeros_like(prev_ref)
        hc_ref[...] = jnp.zeros_like(hc_ref)

    x = xb_ref[...]
    prev = prev_ref[...]
    xc = cb_ref[...] + cw_ref[3:4, :] * x
    for j in range(1, LRU_CONV):
        xc = xc + cw_ref[LRU_CONV - 1 - j:LRU_CONV - j, :] * _shift_rows(x, prev, j)
    prev_ref[...] = x[ts - SUBLANES:ts, :]

    xcb = xc.astype(BF16)
    ga, gx = [], []
    for t in range(c // MXU_TILE):
        blk = xcb[:, t * MXU_TILE:(t + 1) * MXU_TILE]
        ga.append(jnp.dot(blk, wa_ref[t], preferred_element_type=F32))
        gx.append(jnp.dot(blk, wx_ref[t], preferred_element_type=F32))
    r = jax.nn.sigmoid(jnp.concatenate(ga, axis=1) + ba_ref[...])
    i = jax.nn.sigmoid(jnp.concatenate(gx, axis=1) + bx_ref[...])

    nl = -lam_ref[...]
    softplus = jnp.maximum(nl, 0.0) + jnp.log1p(jnp.exp(-jnp.abs(nl)))
    log_a = (-LRU_C) * r * softplus
    a = jnp.exp(log_a)
    gain = jnp.sqrt(-jnp.tanh(log_a) * (a * a + 1.0))
    bt = gain * (i * xc)

    a3 = a.reshape(ts // SUBLANES, SUBLANES, c)
    b3 = bt.reshape(ts // SUBLANES, SUBLANES, c)
    rid = lax.broadcasted_iota(I32, (1, SUBLANES, c), 1)
    for d in (1, 2, 4):
        a_sh = jnp.where(rid >= d, pltpu.roll(a3, d, 1), 1.0)
        b_sh = jnp.where(rid >= d, pltpu.roll(b3, d, 1), 0.0)
        b3 = a3 * b_sh + b3
        a3 = a3 * a_sh
    a_s[...] = a3.reshape(ts, c)
    b_s[...] = b3.reshape(ts, c)

    def body(g, hc):
        r0 = pl.multiple_of(g * SUBLANES, SUBLANES)
        h = a_s[pl.ds(r0, SUBLANES), :] * hc + b_s[pl.ds(r0, SUBLANES), :]
        b_s[pl.ds(r0, SUBLANES), :] = h
        return jnp.broadcast_to(h[SUBLANES - 1:SUBLANES, :], (SUBLANES, c))

    hc_ref[...] = lax.fori_loop(0, ts // SUBLANES, body, hc_ref[...])
    y_ref[...] = (b_s[...] * jax.nn.gelu(gb_ref[...])).astype(y_ref.dtype)


def _lru_core(xb, gb, cw, cb, wa, ba, wx, bx, lam, batch, seq):
    c = LRU_W
    ts = LRU_TILE
    nt = seq // ts
    row = lambda b, j: (b * nt + j, 0)
    const2 = lambda b, j: (0, 0)
    const3 = lambda b, j: (0, 0, 0)
    return pl.pallas_call(
        _lru_kernel,
        grid=(batch, nt),
        in_specs=[
            pl.BlockSpec((ts, c), row),
            pl.BlockSpec((ts, c), row),
            pl.BlockSpec((LRU_CONV, c), const2),
            pl.BlockSpec((1, c), const2),
            pl.BlockSpec((c // MXU_TILE, MXU_TILE, MXU_TILE), const3),
            pl.BlockSpec((1, c), const2),
            pl.BlockSpec((c // MXU_TILE, MXU_TILE, MXU_TILE), const3),
            pl.BlockSpec((1, c), const2),
            pl.BlockSpec((1, c), const2),
        ],
        out_specs=pl.BlockSpec((ts, c), row),
        out_shape=jax.ShapeDtypeStruct((batch * seq, c), BF16),
        scratch_shapes=[
            pltpu.VMEM((SUBLANES, c), F32),
            pltpu.VMEM((SUBLANES, c), F32),
            pltpu.VMEM((ts, c), F32),
            pltpu.VMEM((ts, c), F32),
        ],
        compiler_params=_params(("arbitrary", "arbitrary")),
        name="lru_core",
    )(xb, gb, cw, cb.reshape(1, c), wa, ba.reshape(1, c), wx, bx.reshape(1, c), lam.reshape(1, c))


def _block_diag_tiles(w):
    per = MXU_TILE // LRU_BW
    w4 = w.reshape(LRU_BLOCKS // per, per, LRU_BW, LRU_BW)
    eye = jnp.eye(per, dtype=w.dtype)
    t = jnp.einsum('gpij,pq->gpiqj', w4, eye)
    return t.reshape(LRU_BLOCKS // per, MXU_TILE, MXU_TILE)


def _mix_out_kernel(a_ref, mq_ref, mk_ref, mv_ref, w_ref, x_ref, o_ref):
    mq = mq_ref[...]
    mk = mk_ref[...]
    mv = mv_ref[...]
    scale = MEM_HD ** -0.5
    heads = []
    for h in range(MEM_HEADS):
        sl = slice(h * MEM_HD, (h + 1) * MEM_HD)
        s = lax.dot_general(mq[:, sl], mk[:, sl], (((1,), (1,)), ((), ())),
                            preferred_element_type=F32) * scale
        s = s - jnp.max(s, axis=-1, keepdims=True)
        e = jnp.exp(s)
        p = e / jnp.sum(e, axis=-1, keepdims=True)
        heads.append(jnp.dot(p.astype(BF16), mv[:, sl], preferred_element_type=F32).astype(BF16))
    cat = jnp.concatenate([a_ref[...]] + heads, axis=1)
    o_ref[...] = x_ref[...] + jnp.dot(cat, w_ref[...], preferred_element_type=F32)


def _mix_out(a, mq, memkv, layer, w_out, x, seq):
    t, d = x.shape
    tm = TOK_TILE
    per = seq // tm
    kin = w_out.shape[0]
    return pl.pallas_call(
        _mix_out_kernel,
        grid=(t // tm,),
        in_specs=[
            pl.BlockSpec((tm, a.shape[1]), lambda i: (i, 0)),
            pl.BlockSpec((tm, MEM_W), lambda i: (i, 0)),
            pl.BlockSpec((MEM_TOKENS, MEM_W), lambda i: (i // per, 2 * layer)),
            pl.BlockSpec((MEM_TOKENS, MEM_W), lambda i: (i // per, 2 * layer + 1)),
            pl.BlockSpec((kin, d), lambda i: (0, 0)),
            pl.BlockSpec((tm, d), lambda i: (i, 0)),
        ],
        out_specs=pl.BlockSpec((tm, d), lambda i: (i, 0)),
        out_shape=jax.ShapeDtypeStruct((t, d), F32),
        compiler_params=_params(("arbitrary",)),
        name="mix_out",
    )(a, mq, memkv, memkv, w_out, x)


def _ffn_kernel(x_ref, g_ref, wup_ref, cw_ref, cb_ref, wdn_ref, fg_ref, o_ref, hn_s, acc_s, uprev_s,
                *, final_norm):
    tm = x_ref.shape[0]
    nch = wdn_ref.shape[0] // FF_CHUNK

    @pl.when(pl.program_id(1) == 0)
    def _():
        uprev_s[...] = jnp.zeros_like(uprev_s)

    hn_s[...] = _rms(x_ref[...], g_ref[...]).astype(BF16)
    acc_s[...] = jnp.zeros_like(acc_s)

    def cols(idx):
        return slice(idx * FF_CHUNK, (idx + 1) * FF_CHUNK)

    def up(idx):
        return jnp.dot(hn_s[...], wup_ref[:, cols(idx)], preferred_element_type=F32)

    def conv(u, idx):
        prev = uprev_s[idx]
        y = cb_ref[:, cols(idx)] + cw_ref[2:3, cols(idx)] * u
        y = y + cw_ref[1:2, cols(idx)] * _shift_rows(u, prev, 1)
        y = y + cw_ref[0:1, cols(idx)] * _shift_rows(u, prev, 2)
        uprev_s[idx] = u[tm - SUBLANES:tm, :]
        return y

    def down(act, c):
        acc_s[...] += jnp.dot(act, wdn_ref[c * FF_CHUNK:(c + 1) * FF_CHUNK, :], preferred_element_type=F32)

    ug, uv = up(0), up(nch)
    act_prev = None
    for c in range(nch):
        if c + 1 < nch:
            ug_next, uv_next = up(c + 1), up(c + 1 + nch)
        if act_prev is not None:
            down(act_prev, c - 1)
        act_prev = (jax.nn.silu(conv(ug, c)) * conv(uv, c + nch)).astype(BF16)
        if c + 1 < nch:
            ug, uv = ug_next, uv_next
    down(act_prev, nch - 1)
    out = x_ref[...] + acc_s[...]
    if final_norm:
        out = _rms(out, fg_ref[...])
    o_ref[...] = out


def _ffn(x, g, w_up, conv_w, conv_b, w_down, fg, batch, seq, final_norm):
    t, d = x.shape
    tm = FFN_TILE
    nt = seq // tm
    n2 = w_up.shape[1]
    row = lambda b, j: (b * nt + j, 0)
    const = lambda b, j: (0, 0)
    return pl.pallas_call(
        functools.partial(_ffn_kernel, final_norm=final_norm),
        grid=(batch, nt),
        in_specs=[
            pl.BlockSpec((tm, d), row),
            pl.BlockSpec((1, d), const),
            pl.BlockSpec(w_up.shape, const),
            pl.BlockSpec(conv_w.shape, const),
            pl.BlockSpec((1, n2), const),
            pl.BlockSpec(w_down.shape, const),
            pl.BlockSpec((1, d), const),
        ],
        out_specs=pl.BlockSpec((tm, d), row),
        out_shape=jax.ShapeDtypeStruct((t, d), F32),
        scratch_shapes=[
            pltpu.VMEM((tm, d), BF16),
            pltpu.VMEM((tm, d), F32),
            pltpu.VMEM((n2 // FF_CHUNK, SUBLANES, FF_CHUNK), F32),
        ],
        compiler_params=_params(("arbitrary", "arbitrary")),
        name="ffn",
    )(x, g.reshape(1, d), w_up.astype(BF16), conv_w, conv_b.reshape(1, n2), w_down.astype(BF16),
      fg.reshape(1, d))


def _dsa_attn_kernel(ik_ref, iq_ref, iw_ref, q_ref, k_ref, vt_ref, o_ref,
                     keys_s, m_s, l_s, acc_s, p_s, alpha_s):
    tq = q_ref.shape[0]
    kc_n = K_CHUNK
    qi = pl.program_id(1)
    nk = qi + 1
    grp = kc_n // SUBLANES

    iw_t = iw_ref[...].T[0:IDX_HEADS, :] * ((IDX_HEADS ** -0.5) * (IDX_HD ** -0.5))
    iq = iq_ref[...]
    lane = lax.broadcasted_iota(I32, (kc_n, LANES), 1)
    krow = lax.broadcasted_iota(I32, (kc_n, tq), 0)
    qcol = lax.broadcasted_iota(I32, (kc_n, tq), 1)

    def score_body(kc, carry):
        r0 = pl.multiple_of(kc * kc_n, kc_n)
        ik2 = ik_ref[pl.ds(r0, kc_n), :]
        ik_lo = jnp.where(lane < IDX_HD, ik2, jnp.zeros_like(ik2))
        ik_hi = jnp.where(lane >= IDX_HD, ik2, jnp.zeros_like(ik2))
        score = jnp.zeros((kc_n, tq), F32)
        for h in range(IDX_HEADS):
            pair = iq[:, (h // 2) * LANES:(h // 2 + 1) * LANES]
            lhs = ik_lo if h % 2 == 0 else ik_hi
            rel = lax.dot_general(lhs, pair, (((1,), (1,)), ((), ())), preferred_element_type=F32)
            score = score + iw_t[h:h + 1, :] * jnp.maximum(rel, 0.0)
        causal = (krow + r0) <= (qcol + qi * tq)
        keys_s[pl.ds(r0, kc_n), :] = jnp.where(causal, score, -jnp.inf)
        return carry

    lax.fori_loop(0, nk, score_body, 0)

    def count(pred_fn):
        def body(kc, part):
            r0 = pl.multiple_of(kc * kc_n, kc_n)
            ones = jnp.where(pred_fn(keys_s[pl.ds(r0, kc_n), :], r0), 1, 0).astype(I32)
            return part + jnp.sum(ones.reshape(kc_n // COUNT_ACC_ROWS, COUNT_ACC_ROWS, tq), axis=0)
        part = lax.fori_loop(0, nk, body, jnp.zeros((COUNT_ACC_ROWS, tq), I32))
        return jnp.sum(part, axis=0, keepdims=True)

    def ordered_to_f32(u):
        neg_inf_u = jnp.int32(0x007FFFFF)
        u = jnp.where((u >= 0) & (u < neg_inf_u), neg_inf_u, u)
        k = u ^ INT_MIN
        return pltpu.bitcast(k ^ ((k >> 31) & jnp.int32(0x7FFFFFFF)), F32)

    def bit_body(t, carry):
        tu, cnt_ge = carry
        cand_u = tu | lax.shift_left(jnp.int32(1), jnp.int32(31) - t)
        cand_f = ordered_to_f32(cand_u)
        cnt = count(lambda kv, r0: kv >= cand_f)
        ok = cnt >= TOPK_MAX
        return jnp.where(ok, cand_u, tu), jnp.where(ok, cnt, cnt_ge)

    zeros = jnp.zeros((1, tq), I32)
    tu, cnt_ge = lax.fori_loop(0, 32, bit_body, (zeros, zeros + nk * kc_n))
    thr = ordered_to_f32(tu)
    short = thr == -jnp.inf
    excess = jnp.where(short, 0, cnt_ge - TOPK_MAX)

    def tie_search():
        need = TOPK_MAX - count(lambda kv, r0: kv > thr)

        def jbit(t, jc):
            cand = jc | lax.shift_left(jnp.int32(1), jnp.int32(11) - t)
            cnt = count(lambda kv, r0: (kv == thr) & ((krow + r0) < cand))
            return jnp.where(cnt <= need, cand, jc)
        return lax.fori_loop(0, 12, jbit, jnp.zeros((1, tq), I32))

    jcut = lax.cond(jnp.max(excess) > 0, tie_search, lambda: jnp.full((1, tq), 4095, I32))

    def bias_body(kc, carry):
        r0 = pl.multiple_of(kc * kc_n, kc_n)
        kv = keys_s[pl.ds(r0, kc_n), :]
        sel = ((kv > thr) | ((kv == thr) & ((krow + r0) < jcut))) & (kv > -jnp.inf)
        keys_s[pl.ds(r0, kc_n), :] = jnp.where(sel, 0.0, NEG_BIG).astype(F32)
        return carry

    lax.fori_loop(0, nk, bias_body, 0)

    m_s[...] = jnp.full_like(m_s, NEG_BIG)
    l_s[...] = jnp.zeros_like(l_s)
    acc_s[...] = jnp.zeros_like(acc_s)

    def softmax_stage(kc):
        slot = kc & 1
        r0 = pl.multiple_of(kc * kc_n, kc_n)
        bias = keys_s[pl.ds(r0, kc_n), :]
        m_all = m_s[...]
        l_all = l_s[...]
        m_rows, l_rows, a_rows = [], [], []
        for h in range(ATT_HEADS):
            sl = slice(h * ATT_HD, (h + 1) * ATT_HD)
            s = lax.dot_general(k_ref[pl.ds(r0, kc_n), sl], q_ref[:, sl], (((1,), (1,)), ((), ())),
                                preferred_element_type=F32) + bias
            m_prev = m_all[h:h + 1, :]
            m_new = jnp.maximum(m_prev, jnp.max(s, axis=0, keepdims=True))
            alpha = jnp.exp2(m_prev - m_new)
            p = jnp.exp2(s - m_new)
            l_rows.append(alpha * l_all[h:h + 1, :] + jnp.sum(p, axis=0, keepdims=True))
            m_rows.append(m_new)
            a_rows.append(alpha)
            p_s[slot, h * kc_n:(h + 1) * kc_n, :] = p.astype(BF16)
        m_s[...] = jnp.concatenate(m_rows, axis=0)
        l_s[...] = jnp.concatenate(l_rows, axis=0)
        alpha_s[slot] = jnp.concatenate(a_rows, axis=0)

    def pv_stage(kc):
        slot = kc & 1
        al = alpha_s[slot]
        for h in range(ATT_HEADS):
            sl = slice(h * ATT_HD, (h + 1) * ATT_HD)
            pv = jnp.dot(vt_ref[kc, sl, :], p_s[slot, h * kc_n:(h + 1) * kc_n, :],
                         preferred_element_type=F32)
            acc_s[sl, :] = al[h:h + 1, :] * acc_s[sl, :] + pv

    def att_body(kc, carry):
        pv_stage(kc - 1)
        softmax_stage(kc)
        return carry

    softmax_stage(jnp.int32(0))
    lax.fori_loop(1, nk, att_body, 0)
    pv_stage(nk - 1)
    for h in range(ATT_HEADS):
        sl = slice(h * ATT_HD, (h + 1) * ATT_HD)
        o_ref[:, sl] = (acc_s[sl, :] / l_s[h:h + 1, :]).T.astype(o_ref.dtype)


def _dsa_attn(ik2, iq, iw, q, k, vt, batch, seq):
    tq = Q_TILE
    nq = seq // tq
    nkc = seq // K_CHUNK
    r3 = lambda a: a.reshape(batch, seq, a.shape[-1])
    tile = lambda b, j: (b, j, 0)
    full = lambda b, j: (b, 0, 0)
    out = pl.pallas_call(
        _dsa_attn_kernel,
        grid=(batch, nq),
        in_specs=[
            pl.BlockSpec((None, seq, LANES), full),
            pl.BlockSpec((None, tq, IDX_HEADS * IDX_HD), tile),
            pl.BlockSpec((None, tq, LANES), tile),
            pl.BlockSpec((None, tq, ATT_W), tile),
            pl.BlockSpec((None, seq, ATT_W), full),
            pl.BlockSpec((None, nkc, ATT_W, K_CHUNK), lambda b, j: (b, 0, 0, 0)),
        ],
        out_specs=pl.BlockSpec((None, tq, ATT_W), tile),
        out_shape=jax.ShapeDtypeStruct((batch, seq, ATT_W), BF16),
        scratch_shapes=[
            pltpu.VMEM((seq, tq), F32),
            pltpu.VMEM((ATT_HEADS, tq), F32),
            pltpu.VMEM((ATT_HEADS, tq), F32),
            pltpu.VMEM((ATT_W, tq), F32),
            pltpu.VMEM((2, ATT_HEADS * K_CHUNK, tq), BF16),
            pltpu.VMEM((2, ATT_HEADS, tq), F32),
        ],
        compiler_params=_params(("arbitrary", "arbitrary")),
        name="dsa_attn",
    )(r3(ik2), r3(iq), r3(iw), r3(q), r3(k), vt.reshape(batch, nkc, ATT_W, K_CHUNK))
    return out.reshape(batch * seq, ATT_W)


def _rope_inv_rows():
    inv_a = ROPE_THETA ** (-jnp.arange(0, ATT_ROT, 2, dtype=F32) / ATT_ROT)
    inv_i = ROPE_THETA ** (-jnp.arange(0, IDX_ROT, 2, dtype=F32) / IDX_ROT)
    row_a = jnp.concatenate([inv_a, inv_a, jnp.zeros((ATT_HD - ATT_ROT,), F32)])
    half = jnp.concatenate([inv_i, inv_i, jnp.zeros((IDX_HD - IDX_ROT,), F32)])
    row_i = jnp.concatenate([half, half])
    return row_a.reshape(1, LANES), row_i.reshape(1, LANES)


def kernel(x, mem, positions, norm_mix, norm_ffn, mem_norm, final_norm, w_mem_kv, w_ffn_up, ffn_conv_w,
           ffn_conv_b, w_ffn_down, lru_w_in, lru_conv_w, lru_conv_b, lru_w_a, lru_b_a, lru_w_x, lru_b_x,
           lru_lambda, lru_w_out, dsa_w_in, dsa_w_out):
    batch, seq, d = x.shape
    t = batch * seq
    xf = x.reshape(t, d)

    w_kv = jnp.concatenate([w_mem_kv[0], w_mem_kv[1]], axis=1).astype(BF16)
    (memkv,) = _norm_proj(mem.reshape(batch * MEM_TOKENS, d), mem_norm, w_kv,
                          (w_kv.shape[1],), (BF16,), "mem_kv")

    xb, gb, mq = _norm_proj(xf, norm_mix[0], lru_w_in[0].astype(BF16),
                            (LRU_W, LRU_W, MEM_W), (F32, F32, BF16), "lru_proj")
    y = _lru_core(xb, gb, lru_conv_w[0], lru_conv_b[0],
                  _block_diag_tiles(lru_w_a[0]).astype(BF16), lru_b_a[0],
                  _block_diag_tiles(lru_w_x[0]).astype(BF16), lru_b_x[0], lru_lambda[0], batch, seq)
    xf = _mix_out(y, mq, memkv, 0, lru_w_out[0].astype(BF16), xf, seq)
    xf = _ffn(xf, norm_ffn[0], w_ffn_up[0], ffn_conv_w[0], ffn_conv_b[0], w_ffn_down[0],
              final_norm, batch, seq, False)

    w = dsa_w_in[0].astype(BF16)
    o = np.cumsum((0, ATT_W, ATT_W, ATT_W, IDX_HEADS * IDX_HD, IDX_HD, IDX_HEADS, MEM_W))
    w_ik = w[:, o[4]:o[5]]
    w_iw = jnp.pad(w[:, o[5]:o[6]], ((0, 0), (0, LANES - IDX_HEADS)))
    w_cat = jnp.concatenate([w[:, :o[4]], w_ik, w_ik, w_iw, w[:, o[6]:o[7]]], axis=1)
    inva, invi = _rope_inv_rows()
    pos = positions.astype(F32).reshape(t, 1)
    q, k, vt, iq, ik2, iw, mq = _dsa_proj(xf, norm_mix[1], w_cat, pos, inva, invi)
    att = _dsa_attn(ik2, iq, iw, q, k, vt, batch, seq)
    xf = _mix_out(att, mq, memkv, 1, dsa_w_out[0].astype(BF16), xf, seq)
    xf = _ffn(xf, norm_ffn[1], w_ffn_up[1], ffn_conv_w[1], ffn_conv_b[1], w_ffn_down[1],
              final_norm, batch, seq, True)
    return xf.reshape(batch, seq, d)
```

```python
import functools

import numpy as np
import jax
import jax.numpy as jnp
from jax import lax
from jax.experimental import pallas as pl
from jax.experimental.pallas import tpu as pltpu

F32 = jnp.float32
BF16 = jnp.bfloat16
I32 = jnp.int32

D_MODEL = 1024
RMS_EPS = 1e-6
ROPE_THETA = 500000.0

LRU_W = 1024
LRU_BLOCKS = 16
LRU_BW = LRU_W // LRU_BLOCKS
LRU_CONV = 4
LRU_C = 8.0

ATT_HEADS = 8
ATT_HD = 128
ATT_W = ATT_HEADS * ATT_HD
ATT_ROT = ATT_HD // 4
IDX_HEADS = 8
IDX_HD = 64
IDX_ROT = IDX_HD // 4
TOPK_MAX = 256

MEM_TOKENS = 256
MEM_HEADS = 4
MEM_HD = 128
MEM_W = MEM_HEADS * MEM_HD

D_FF = 2816
FFN_CONV = 3

SUBLANES = 8
LANES = 128
MXU_TILE = 256
VMEM_LIMIT_BYTES = 56 * 1024 * 1024

INT_MIN = np.int32(-2 ** 31)
NEG_BIG = -1e30
LOG2_E = 1.4426950408889634

TOK_TILE = 512
FFN_TILE = 512
LRU_TILE = 512
Q_TILE = 256
K_CHUNK = 256
FF_CHUNK = 256
COUNT_ACC_ROWS = 16


def _params(sem):
    return pltpu.CompilerParams(dimension_semantics=sem, vmem_limit_bytes=VMEM_LIMIT_BYTES)


def _rms(x, g):
    ms = jnp.mean(x * x, axis=-1, keepdims=True)
    return x * lax.rsqrt(ms + RMS_EPS) * g


def _shift_rows(x, prev, j):
    r = pltpu.roll(x, j, 0)
    p = pltpu.roll(prev, j, 0)
    rid = lax.broadcasted_iota(I32, (SUBLANES, x.shape[1]), 0)
    top = jnp.where(rid < j, p, r[0:SUBLANES])
    return jnp.concatenate([top, r[SUBLANES:]], axis=0)


def _norm_proj_kernel(x_ref, g_ref, w_ref, *out_refs, splits):
    hn = _rms(x_ref[...], g_ref[...]).astype(BF16)
    z = jnp.dot(hn, w_ref[...], preferred_element_type=F32)
    off = 0
    for o_ref, n in zip(out_refs, splits):
        o_ref[...] = z[:, off:off + n].astype(o_ref.dtype)
        off += n


def _norm_proj(x, g, w, splits, dtypes, name):
    t, d = x.shape
    n = w.shape[1]
    tm = min(TOK_TILE, t)
    return pl.pallas_call(
        functools.partial(_norm_proj_kernel, splits=splits),
        grid=(t // tm,),
        in_specs=[
            pl.BlockSpec((tm, d), lambda i: (i, 0)),
            pl.BlockSpec((1, d), lambda i: (0, 0)),
            pl.BlockSpec((d, n), lambda i: (0, 0)),
        ],
        out_specs=[pl.BlockSpec((tm, s), lambda i: (i, 0)) for s in splits],
        out_shape=[jax.ShapeDtypeStruct((t, s), dt) for s, dt in zip(splits, dtypes)],
        compiler_params=_params(("arbitrary",)),
        name=name,
    )(x, g.reshape(1, d), w)


DSA_SPLITS = (ATT_W, ATT_W, ATT_W, IDX_HEADS * IDX_HD, LANES, LANES, MEM_W)


def _rope(xh, c, s_lo, s_hi, half):
    return xh * c + pltpu.roll(xh, LANES - half, 1) * s_lo + pltpu.roll(xh, half, 1) * s_hi


def _dsa_proj_kernel(x_ref, g_ref, w_ref, pos_ref, inva_ref, invi_ref,
                     q_ref, k_ref, v_ref, iq_ref, ik_ref, iw_ref, mq_ref):
    hn = _rms(x_ref[...], g_ref[...]).astype(BF16)
    z = jnp.dot(hn, w_ref[...], preferred_element_type=F32)
    tm = z.shape[0]
    pos = pos_ref[...]
    lane = lax.broadcasted_iota(I32, (tm, LANES), 1)
    ang_a = pos * inva_ref[...]
    ca, sa = jnp.cos(ang_a), jnp.sin(ang_a)
    ha = ATT_ROT // 2
    sa_lo = jnp.where(lane < ha, -sa, 0.0)
    sa_hi = jnp.where((lane >= ha) & (lane < 2 * ha), sa, 0.0)
    ang_i = pos * invi_ref[...]
    ci, si = jnp.cos(ang_i), jnp.sin(ang_i)
    hi = IDX_ROT // 2
    m64 = lane & (IDX_HD - 1)
    si_lo = jnp.where(m64 < hi, -si, 0.0)
    si_hi = jnp.where((m64 >= hi) & (m64 < 2 * hi), si, 0.0)

    off = 0
    for h in range(ATT_HEADS):
        sl = slice(off + h * ATT_HD, off + (h + 1) * ATT_HD)
        qh = _rope(z[:, sl], ca, sa_lo, sa_hi, ha) * (ATT_HD ** -0.5 * LOG2_E)
        q_ref[:, h * ATT_HD:(h + 1) * ATT_HD] = qh.astype(q_ref.dtype)
    off += ATT_W
    for h in range(ATT_HEADS):
        sl = slice(off + h * ATT_HD, off + (h + 1) * ATT_HD)
        k_ref[:, h * ATT_HD:(h + 1) * ATT_HD] = _rope(z[:, sl], ca, sa_lo, sa_hi, ha).astype(k_ref.dtype)
    off += ATT_W
    for j in range(tm // K_CHUNK):
        v_ref[j] = z[j * K_CHUNK:(j + 1) * K_CHUNK, off:off + ATT_W].T.astype(v_ref.dtype)
    off += ATT_W
    for p in range(IDX_HEADS * IDX_HD // LANES):
        sl = slice(off + p * LANES, off + (p + 1) * LANES)
        iq_ref[:, p * LANES:(p + 1) * LANES] = _rope(z[:, sl], ci, si_lo, si_hi, hi).astype(iq_ref.dtype)
    off += IDX_HEADS * IDX_HD
    ik_ref[...] = _rope(z[:, off:off + LANES], ci, si_lo, si_hi, hi).astype(ik_ref.dtype)
    off += LANES
    iw_ref[...] = z[:, off:off + LANES].astype(iw_ref.dtype)
    off += LANES
    mq_ref[...] = z[:, off:off + MEM_W].astype(mq_ref.dtype)


def _dsa_proj(x, g, w, pos, inva, invi):
    t, d = x.shape
    n = w.shape[1]
    tm = TOK_TILE
    dtypes = (BF16, BF16, BF16, BF16, BF16, F32, BF16)
    return pl.pallas_call(
        _dsa_proj_kernel,
        grid=(t // tm,),
        in_specs=[
            pl.BlockSpec((tm, d), lambda i: (i, 0)),
            pl.BlockSpec((1, d), lambda i: (0, 0)),
            pl.BlockSpec((d, n), lambda i: (0, 0)),
            pl.BlockSpec((tm, 1), lambda i: (i, 0)),
            pl.BlockSpec((1, LANES), lambda i: (0, 0)),
            pl.BlockSpec((1, LANES), lambda i: (0, 0)),
        ],
        out_specs=[pl.BlockSpec((tm // K_CHUNK, ATT_W, K_CHUNK), lambda i: (i, 0, 0)) if j == 2
                   else pl.BlockSpec((tm, s), lambda i: (i, 0)) for j, s in enumerate(DSA_SPLITS)],
        out_shape=[jax.ShapeDtypeStruct((t // K_CHUNK, ATT_W, K_CHUNK), dt) if j == 2
                   else jax.ShapeDtypeStruct((t, s), dt) for j, (s, dt) in enumerate(zip(DSA_SPLITS, dtypes))],
        compiler_params=_params(("arbitrary",)),
        name="dsa_proj",
    )(x, g.reshape(1, d), w, pos, inva, invi)


def _lru_kernel(xb_ref, gb_ref, cw_ref, cb_ref, wa_ref, ba_ref, wx_ref, bx_ref, lam_ref, y_ref,
                prev_ref, hc_ref, a_s, b_s):
    ts, c = xb_ref.shape

    @pl.when(pl.program_id(1) == 0)
    def _():
        prev_ref[...] = jnp.zeros_like(prev_ref)
        hc_ref[...] = jnp.zeros_like(hc_ref)

    x = xb_ref[...]
    prev = prev_ref[...]
    xc = cb_ref[...] + cw_ref[3:4, :] * x
    for j in range(1, LRU_CONV):
        xc = xc + cw_ref[LRU_CONV - 1 - j:LRU_CONV - j, :] * _shift_rows(x, prev, j)
    prev_ref[...] = x[ts - SUBLANES:ts, :]

    xcb = xc.astype(BF16)
    ga, gx = [], []
    for t in range(c // MXU_TILE):
        blk = xcb[:, t * MXU_TILE:(t + 1) * MXU_TILE]
        ga.append(jnp.dot(blk, wa_ref[t], preferred_element_type=F32))
        gx.append(jnp.dot(blk, wx_ref[t], preferred_element_type=F32))
    r = jax.nn.sigmoid(jnp.concatenate(ga, axis=1) + ba_ref[...])
    i = jax.nn.sigmoid(jnp.concatenate(gx, axis=1) + bx_ref[...])

    nl = -lam_ref[...]
    softplus = jnp.maximum(nl, 0.0) + jnp.log1p(jnp.exp(-jnp.abs(nl)))
    log_a = (-LRU_C) * r * softplus
    a = jnp.exp(log_a)
    gain = jnp.sqrt(-jnp.tanh(log_a) * (a * a + 1.0))
    bt = gain * (i * xc)

    a3 = a.reshape(ts // SUBLANES, SUBLANES, c)
    b3 = bt.reshape(ts // SUBLANES, SUBLANES, c)
    rid = lax.broadcasted_iota(I32, (1, SUBLANES, c), 1)
    for d in (1, 2, 4):
        a_sh = jnp.where(rid >= d, pltpu.roll(a3, d, 1), 1.0)
        b_sh = jnp.where(rid >= d, pltpu.roll(b3, d, 1), 0.0)
        b3 = a3 * b_sh + b3
        a3 = a3 * a_sh
    a_s[...] = a3.reshape(ts, c)
    b_s[...] = b3.reshape(ts, c)

    def body(g, hc):
        r0 = pl.multiple_of(g * SUBLANES, SUBLANES)
        h = a_s[pl.ds(r0, SUBLANES), :] * hc + b_s[pl.ds(r0, SUBLANES), :]
        b_s[pl.ds(r0, SUBLANES), :] = h
        return jnp.broadcast_to(h[SUBLANES - 1:SUBLANES, :], (SUBLANES, c))

    hc_ref[...] = lax.fori_loop(0, ts // SUBLANES, body, hc_ref[...])
    y_ref[...] = (b_s[...] * jax.nn.gelu(gb_ref[...])).astype(y_ref.dtype)


def _lru_core(xb, gb, cw, cb, wa, ba, wx, bx, lam, batch, seq):
    c = LRU_W
    ts = LRU_TILE
    nt = seq // ts
    row = lambda b, j: (b * nt + j, 0)
    const2 = lambda b, j: (0, 0)
    const3 = lambda b, j: (0, 0, 0)
    return pl.pallas_call(
        _lru_kernel,
        grid=(batch, nt),
        in_specs=[
            pl.BlockSpec((ts, c), row),
            pl.BlockSpec((ts, c), row),
            pl.BlockSpec((LRU_CONV, c), const2),
            pl.BlockSpec((1, c), const2),
            pl.BlockSpec((c // MXU_TILE, MXU_TILE, MXU_TILE), const3),
            pl.BlockSpec((1, c), const2),
            pl.BlockSpec((c // MXU_TILE, MXU_TILE, MXU_TILE), const3),
            pl.BlockSpec((1, c), const2),
            pl.BlockSpec((1, c), const2),
        ],
        out_specs=pl.BlockSpec((ts, c), row),
        out_shape=jax.ShapeDtypeStruct((batch * seq, c), BF16),
        scratch_shapes=[
            pltpu.VMEM((SUBLANES, c), F32),
            pltpu.VMEM((SUBLANES, c), F32),
            pltpu.VMEM((ts, c), F32),
            pltpu.VMEM((ts, c), F32),
        ],
        compiler_params=_params(("arbitrary", "arbitrary")),
        name="lru_core",
    )(xb, gb, cw, cb.reshape(1, c), wa, ba.reshape(1, c), wx, bx.reshape(1, c), lam.reshape(1, c))


def _block_diag_tiles(w):
    per = MXU_TILE // LRU_BW
    w4 = w.reshape(LRU_BLOCKS // per, per, LRU_BW, LRU_BW)
    eye = jnp.eye(per, dtype=w.dtype)
    t = jnp.einsum('gpij,pq->gpiqj', w4, eye)
    return t.reshape(LRU_BLOCKS // per, MXU_TILE, MXU_TILE)


def _mix_out_kernel(a_ref, mq_ref, mk_ref, mv_ref, w_ref, x_ref, o_ref):
    mq = mq_ref[...]
    mk = mk_ref[...]
    mv = mv_ref[...]
    scale = MEM_HD ** -0.5
    n_a = a_ref.shape[1]
    scores = []
    for h in range(MEM_HEADS):
        sl = slice(h * MEM_HD, (h + 1) * MEM_HD)
        scores.append(lax.dot_general(mq[:, sl], mk[:, sl], (((1,), (1,)), ((), ())),
                                      preferred_element_type=F32))
    acc = jnp.dot(a_ref[...], w_ref[0:n_a, :], preferred_element_type=F32)
    heads = []
    for h in range(MEM_HEADS):
        sl = slice(h * MEM_HD, (h + 1) * MEM_HD)
        s = scores[h] * scale
        s = s - jnp.max(s, axis=-1, keepdims=True)
        e = jnp.exp(s)
        p = e / jnp.sum(e, axis=-1, keepdims=True)
        heads.append(jnp.dot(p.astype(BF16), mv[:, sl], preferred_element_type=F32).astype(BF16))
    acc = acc + jnp.dot(jnp.concatenate(heads, axis=1), w_ref[n_a:, :], preferred_element_type=F32)
    o_ref[...] = x_ref[...] + acc


def _mix_out(a, mq, memkv, layer, w_out, x, seq):
    t, d = x.shape
    tm = TOK_TILE
    per = seq // tm
    kin = w_out.shape[0]
    return pl.pallas_call(
        _mix_out_kernel,
        grid=(t // tm,),
        in_specs=[
            pl.BlockSpec((tm, a.shape[1]), lambda i: (i, 0)),
            pl.BlockSpec((tm, MEM_W), lambda i: (i, 0)),
            pl.BlockSpec((MEM_TOKENS, MEM_W), lambda i: (i // per, 2 * layer)),
            pl.BlockSpec((MEM_TOKENS, MEM_W), lambda i: (i // per, 2 * layer + 1)),
            pl.BlockSpec((kin, d), lambda i: (0, 0)),
            pl.BlockSpec((tm, d), lambda i: (i, 0)),
        ],
        out_specs=pl.BlockSpec((tm, d), lambda i: (i, 0)),
        out_shape=jax.ShapeDtypeStruct((t, d), F32),
        compiler_params=_params(("arbitrary",)),
        name="mix_out",
    )(a, mq, memkv, memkv, w_out, x)


def _ffn_kernel(x_ref, g_ref, wup_ref, cw_ref, cb_ref, wdn_ref, fg_ref, o_ref, hn_s, acc_s, uprev_s,
                *, final_norm):
    tm = x_ref.shape[0]
    nch = wdn_ref.shape[0] // FF_CHUNK

    @pl.when(pl.program_id(1) == 0)
    def _():
        uprev_s[...] = jnp.zeros_like(uprev_s)

    hn_s[...] = _rms(x_ref[...], g_ref[...]).astype(BF16)
    acc_s[...] = jnp.zeros_like(acc_s)

    def cols(idx):
        return slice(idx * FF_CHUNK, (idx + 1) * FF_CHUNK)

    def up(idx):
        return jnp.dot(hn_s[...], wup_ref[:, cols(idx)], preferred_element_type=F32)

    def conv(u, idx):
        prev = uprev_s[idx]
        y = cb_ref[:, cols(idx)] + cw_ref[2:3, cols(idx)] * u
        y = y + cw_ref[1:2, cols(idx)] * _shift_rows(u, prev, 1)
        y = y + cw_ref[0:1, cols(idx)] * _shift_rows(u, prev, 2)
        uprev_s[idx] = u[tm - SUBLANES:tm, :]
        return y

    def down(act, c):
        acc_s[...] += jnp.dot(act, wdn_ref[c * FF_CHUNK:(c + 1) * FF_CHUNK, :], preferred_element_type=F32)

    ug, uv = up(0), up(nch)
    act_prev = None
    for c in range(nch):
        if c + 1 < nch:
            ug_next, uv_next = up(c + 1), up(c + 1 + nch)
        if act_prev is not None:
            down(act_prev, c - 1)
        act_prev = (jax.nn.silu(conv(ug, c)) * conv(uv, c + nch)).astype(BF16)
        if c + 1 < nch:
            ug, uv = ug_next, uv_next
    down(act_prev, nch - 1)
    out = x_ref[...] + acc_s[...]
    if final_norm:
        out = _rms(out, fg_ref[...])
    o_ref[...] = out


def _ffn(x, g, w_up, conv_w, conv_b, w_down, fg, batch, seq, final_norm):
    t, d = x.shape
    tm = FFN_TILE
    nt = seq // tm
    n2 = w_up.shape[1]
    row = lambda b, j: (b * nt + j, 0)
    const = lambda b, j: (0, 0)
    return pl.pallas_call(
        functools.partial(_ffn_kernel, final_norm=final_norm),
        grid=(batch, nt),
        in_specs=[
            pl.BlockSpec((tm, d), row),
            pl.BlockSpec((1, d), const),
            pl.BlockSpec(w_up.shape, const),
            pl.BlockSpec(conv_w.shape, const),
            pl.BlockSpec((1, n2), const),
            pl.BlockSpec(w_down.shape, const),
            pl.BlockSpec((1, d), const),
        ],
        out_specs=pl.BlockSpec((tm, d), row),
        out_shape=jax.ShapeDtypeStruct((t, d), F32),
        scratch_shapes=[
            pltpu.VMEM((tm, d), BF16),
            pltpu.VMEM((tm, d), F32),
            pltpu.VMEM((n2 // FF_CHUNK, SUBLANES, FF_CHUNK), F32),
        ],
        compiler_params=_params(("arbitrary", "arbitrary")),
        name="ffn",
    )(x, g.reshape(1, d), w_up.astype(BF16), conv_w, conv_b.reshape(1, n2), w_down.astype(BF16),
      fg.reshape(1, d))


def _dsa_attn_kernel(ik_ref, iq_ref, iw_ref, q_ref, k_ref, vt_ref, o_ref,
                     keys_s, m_s, l_s, acc_s, p_s, alpha_s):
    tq = q_ref.shape[0]
    kc_n = K_CHUNK
    qi = pl.program_id(1)
    nk = qi + 1
    grp = kc_n // SUBLANES

    iw_t = iw_ref[...].T[0:IDX_HEADS, :] * ((IDX_HEADS ** -0.5) * (IDX_HD ** -0.5))
    iq = iq_ref[...]
    lane = lax.broadcasted_iota(I32, (kc_n, LANES), 1)
    krow = lax.broadcasted_iota(I32, (kc_n, tq), 0)
    qcol = lax.broadcasted_iota(I32, (kc_n, tq), 1)

    def score_body(kc, carry):
        r0 = pl.multiple_of(kc * kc_n, kc_n)
        ik2 = ik_ref[pl.ds(r0, kc_n), :]
        ik_lo = jnp.where(lane < IDX_HD, ik2, jnp.zeros_like(ik2))
        ik_hi = jnp.where(lane >= IDX_HD, ik2, jnp.zeros_like(ik2))
        score = jnp.zeros((kc_n, tq), F32)
        for h in range(IDX_HEADS):
            pair = iq[:, (h // 2) * LANES:(h // 2 + 1) * LANES]
            lhs = ik_lo if h % 2 == 0 else ik_hi
            rel = lax.dot_general(lhs, pair, (((1,), (1,)), ((), ())), preferred_element_type=F32)
            score = score + iw_t[h:h + 1, :] * jnp.maximum(rel, 0.0)
        causal = (krow + r0) <= (qcol + qi * tq)
        keys_s[pl.ds(r0, kc_n), :] = jnp.where(causal, score, -jnp.inf)
        return carry

    lax.fori_loop(0, nk, score_body, 0)

    def count(pred_fn):
        def body(kc, part):
            r0 = pl.multiple_of(kc * kc_n, kc_n)
            ones = jnp.where(pred_fn(keys_s[pl.ds(r0, kc_n), :], r0), 1, 0).astype(I32)
            return part + jnp.sum(ones.reshape(kc_n // COUNT_ACC_ROWS, COUNT_ACC_ROWS, tq), axis=0)
        part = lax.fori_loop(0, nk, body, jnp.zeros((COUNT_ACC_ROWS, tq), I32))
        return jnp.sum(part, axis=0, keepdims=True)

    def ordered_to_f32(u):
        neg_inf_u = jnp.int32(0x007FFFFF)
        u = jnp.where((u >= 0) & (u < neg_inf_u), neg_inf_u, u)
        k = u ^ INT_MIN
        return pltpu.bitcast(k ^ ((k >> 31) & jnp.int32(0x7FFFFFFF)), F32)

    def bit_body(t, carry):
        tu, cnt_ge = carry
        cand_u = tu | lax.shift_left(jnp.int32(1), jnp.int32(31) - t)
        cand_f = ordered_to_f32(cand_u)
        cnt = count(lambda kv, r0: kv >= cand_f)
        ok = cnt >= TOPK_MAX
        return jnp.where(ok, cand_u, tu), jnp.where(ok, cnt, cnt_ge)

    zeros = jnp.zeros((1, tq), I32)
    tu, cnt_ge = lax.fori_loop(0, 32, bit_body, (zeros, zeros + nk * kc_n))
    thr = ordered_to_f32(tu)
    short = thr == -jnp.inf
    excess = jnp.where(short, 0, cnt_ge - TOPK_MAX)

    def tie_search():
        need = TOPK_MAX - count(lambda kv, r0: kv > thr)

        def jbit(t, jc):
            cand = jc | lax.shift_left(jnp.int32(1), jnp.int32(11) - t)
            cnt = count(lambda kv, r0: (kv == thr) & ((krow + r0) < cand))
            return jnp.where(cnt <= need, cand, jc)
        return lax.fori_loop(0, 12, jbit, jnp.zeros((1, tq), I32))

    jcut = lax.cond(jnp.max(excess) > 0, tie_search, lambda: jnp.full((1, tq), 4095, I32))

    def bias_body(kc, carry):
        r0 = pl.multiple_of(kc * kc_n, kc_n)
        kv = keys_s[pl.ds(r0, kc_n), :]
        sel = ((kv > thr) | ((kv == thr) & ((krow + r0) < jcut))) & (kv > -jnp.inf)
        keys_s[pl.ds(r0, kc_n), :] = jnp.where(sel, 0.0, NEG_BIG).astype(F32)
        return carry

    lax.fori_loop(0, nk, bias_body, 0)

    m_s[...] = jnp.full_like(m_s, NEG_BIG)
    l_s[...] = jnp.zeros_like(l_s)
    acc_s[...] = jnp.zeros_like(acc_s)

    def softmax_stage(kc):
        slot = kc & 1
        r0 = pl.multiple_of(kc * kc_n, kc_n)
        bias = keys_s[pl.ds(r0, kc_n), :]
        m_all = m_s[...]
        l_all = l_s[...]
        m_rows, l_rows, a_rows = [], [], []
        for h in range(ATT_HEADS):
            sl = slice(h * ATT_HD, (h + 1) * ATT_HD)
            s = lax.dot_general(k_ref[pl.ds(r0, kc_n), sl], q_ref[:, sl], (((1,), (1,)), ((), ())),
                                preferred_element_type=F32) + bias
            m_prev = m_all[h:h + 1, :]
            m_new = jnp.maximum(m_prev, jnp.max(s, axis=0, keepdims=True))
            alpha = jnp.exp2(m_prev - m_new)
            p = jnp.exp2(s - m_new)
            l_rows.append(alpha * l_all[h:h + 1, :] + jnp.sum(p, axis=0, keepdims=True))
            m_rows.append(m_new)
            a_rows.append(alpha)
            p_s[slot, h * kc_n:(h + 1) * kc_n, :] = p.astype(BF16)
        m_s[...] = jnp.concatenate(m_rows, axis=0)
        l_s[...] = jnp.concatenate(l_rows, axis=0)
        alpha_s[slot] = jnp.concatenate(a_rows, axis=0)

    def pv_stage(kc):
        slot = kc & 1
        al = alpha_s[slot]
        for h in range(ATT_HEADS):
            sl = slice(h * ATT_HD, (h + 1) * ATT_HD)
            pv = jnp.dot(vt_ref[kc, sl, :], p_s[slot, h * kc_n:(h + 1) * kc_n, :],
                         preferred_element_type=F32)
            acc_s[sl, :] = al[h:h + 1, :] * acc_s[sl, :] + pv

    def att_body(kc, carry):
        pv_stage(kc - 1)
        softmax_stage(kc)
        return carry

    softmax_stage(jnp.int32(0))
    lax.fori_loop(1, nk, att_body, 0)
    pv_stage(nk - 1)
    for h in range(ATT_HEADS):
        sl = slice(h * ATT_HD, (h + 1) * ATT_HD)
        o_ref[:, sl] = (acc_s[sl, :] / l_s[h:h + 1, :]).T.astype(o_ref.dtype)


def _dsa_attn(ik2, iq, iw, q, k, vt, batch, seq):
    tq = Q_TILE
    nq = seq // tq
    nkc = seq // K_CHUNK
    r3 = lambda a: a.reshape(batch, seq, a.shape[-1])
    tile = lambda b, j: (b, j, 0)
    full = lambda b, j: (b, 0, 0)
    out = pl.pallas_call(
        _dsa_attn_kernel,
        grid=(batch, nq),
        in_specs=[
            pl.BlockSpec((None, seq, LANES), full),
            pl.BlockSpec((None, tq, IDX_HEADS * IDX_HD), tile),
            pl.BlockSpec((None, tq, LANES), tile),
            pl.BlockSpec((None, tq, ATT_W), tile),
            pl.BlockSpec((None, seq, ATT_W), full),
            pl.BlockSpec((None, nkc, ATT_W, K_CHUNK), lambda b, j: (b, 0, 0, 0)),
        ],
        out_specs=pl.BlockSpec((None, tq, ATT_W), tile),
        out_shape=jax.ShapeDtypeStruct((batch, seq, ATT_W), BF16),
        scratch_shapes=[
            pltpu.VMEM((seq, tq), F32),
            pltpu.VMEM((ATT_HEADS, tq), F32),
            pltpu.VMEM((ATT_HEADS, tq), F32),
            pltpu.VMEM((ATT_W, tq), F32),
            pltpu.VMEM((2, ATT_HEADS * K_CHUNK, tq), BF16),
            pltpu.VMEM((2, ATT_HEADS, tq), F32),
        ],
        compiler_params=_params(("arbitrary", "arbitrary")),
        name="dsa_attn",
    )(r3(ik2), r3(iq), r3(iw), r3(q), r3(k), vt.reshape(batch, nkc, ATT_W, K_CHUNK))
    return out.reshape(batch * seq, ATT_W)


def _rope_inv_rows():
    inv_a = ROPE_THETA ** (-jnp.arange(0, ATT_ROT, 2, dtype=F32) / ATT_ROT)
    inv_i = ROPE_THETA ** (-jnp.arange(0, IDX_ROT, 2, dtype=F32) / IDX_ROT)
    row_a = jnp.concatenate([inv_a, inv_a, jnp.zeros((ATT_HD - ATT_ROT,), F32)])
    half = jnp.concatenate([inv_i, inv_i, jnp.zeros((IDX_HD - IDX_ROT,), F32)])
    row_i = jnp.concatenate([half, half])
    return row_a.reshape(1, LANES), row_i.reshape(1, LANES)


def kernel(x, mem, positions, norm_mix, norm_ffn, mem_norm, final_norm, w_mem_kv, w_ffn_up, ffn_conv_w,
           ffn_conv_b, w_ffn_down, lru_w_in, lru_conv_w, lru_conv_b, lru_w_a, lru_b_a, lru_w_x, lru_b_x,
           lru_lambda, lru_w_out, dsa_w_in, dsa_w_out):
    batch, seq, d = x.shape
    t = batch * seq
    xf = x.reshape(t, d)

    w_kv = jnp.concatenate([w_mem_kv[0], w_mem_kv[1]], axis=1).astype(BF16)
    (memkv,) = _norm_proj(mem.reshape(batch * MEM_TOKENS, d), mem_norm, w_kv,
                          (w_kv.shape[1],), (BF16,), "mem_kv")

    xb, gb, mq = _norm_proj(xf, norm_mix[0], lru_w_in[0].astype(BF16),
                            (LRU_W, LRU_W, MEM_W), (F32, F32, BF16), "lru_proj")
    y = _lru_core(xb, gb, lru_conv_w[0], lru_conv_b[0],
                  _block_diag_tiles(lru_w_a[0]).astype(BF16), lru_b_a[0],
                  _block_diag_tiles(lru_w_x[0]).astype(BF16), lru_b_x[0], lru_lambda[0], batch, seq)
    xf = _mix_out(y, mq, memkv, 0, lru_w_out[0].astype(BF16), xf, seq)
    xf = _ffn(xf, norm_ffn[0], w_ffn_up[0], ffn_conv_w[0], ffn_conv_b[0], w_ffn_down[0],
              final_norm, batch, seq, False)

    w = dsa_w_in[0].astype(BF16)
    o = np.cumsum((0, ATT_W, ATT_W, ATT_W, IDX_HEADS * IDX_HD, IDX_HD, IDX_HEADS, MEM_W))
    w_ik = w[:, o[4]:o[5]]
    w_iw = jnp.pad(w[:, o[5]:o[6]], ((0, 0), (0, LANES - IDX_HEADS)))
    w_cat = jnp.concatenate([w[:, :o[4]], w_ik, w_ik, w_iw, w[:, o[6]:o[7]]], axis=1)
    inva, invi = _rope_inv_rows()
    pos = positions.astype(F32).reshape(t, 1)
    q, k, vt, iq, ik2, iw, mq = _dsa_proj(xf, norm_mix[1], w_cat, pos, inva, invi)
    att = _dsa_attn(ik2, iq, iw, q, k, vt, batch, seq)
    xf = _mix_out(att, mq, memkv, 1, dsa_w_out[0].astype(BF16), xf, seq)
    xf = _ffn(xf, norm_ffn[1], w_ffn_up[1], ffn_conv_w[1], ffn_conv_b[1], w_ffn_down[1],
              final_norm, batch, seq, True)
    return xf.reshape(batch, seq, d)
```

```python
import functools

import numpy as np
import jax
import jax.numpy as jnp
from jax import lax
from jax.experimental import pallas as pl
from jax.experimental.pallas import tpu as pltpu

F32 = jnp.float32
BF16 = jnp.bfloat16
I32 = jnp.int32

D_MODEL = 1024
RMS_EPS = 1e-6
ROPE_THETA = 500000.0

LRU_W = 1024
LRU_BLOCKS = 16
LRU_BW = LRU_W // LRU_BLOCKS
LRU_CONV = 4
LRU_C = 8.0

ATT_HEADS = 8
ATT_HD = 128
ATT_W = ATT_HEADS * ATT_HD
ATT_ROT = ATT_HD // 4
IDX_HEADS = 8
IDX_HD = 64
IDX_ROT = IDX_HD // 4
TOPK_MAX = 256

MEM_TOKENS = 256
MEM_HEADS = 4
MEM_HD = 128
MEM_W = MEM_HEADS * MEM_HD

D_FF = 2816
FFN_CONV = 3

SUBLANES = 8
LANES = 128
MXU_TILE = 256
VMEM_LIMIT_BYTES = 56 * 1024 * 1024

INT_MIN = np.int32(-2 ** 31)
NEG_BIG = -1e30
LOG2_E = 1.4426950408889634

TOK_TILE = 512
FFN_TILE = 512
LRU_TILE = 512
Q_TILE = 256
K_CHUNK = 256
FF_CHUNK = 256
COUNT_ACC_ROWS = 16


def _params(sem):
    return pltpu.CompilerParams(dimension_semantics=sem, vmem_limit_bytes=VMEM_LIMIT_BYTES)


def _rms(x, g):
    ms = jnp.mean(x * x, axis=-1, keepdims=True)
    return x * lax.rsqrt(ms + RMS_EPS) * g


def _shift_rows(x, prev, j):
    r = pltpu.roll(x, j, 0)
    p = pltpu.roll(prev, j, 0)
    rid = lax.broadcasted_iota(I32, (SUBLANES, x.shape[1]), 0)
    top = jnp.where(rid < j, p, r[0:SUBLANES])
    return jnp.concatenate([top, r[SUBLANES:]], axis=0)


def _norm_proj_kernel(x_ref, g_ref, w_ref, *out_refs, splits):
    hn = _rms(x_ref[...], g_ref[...]).astype(BF16)
    z = jnp.dot(hn, w_ref[...], preferred_element_type=F32)
    off = 0
    for o_ref, n in zip(out_refs, splits):
        o_ref[...] = z[:, off:off + n].astype(o_ref.dtype)
        off += n


def _norm_proj(x, g, w, splits, dtypes, name):
    t, d = x.shape
    n = w.shape[1]
    tm = min(TOK_TILE, t)
    return pl.pallas_call(
        functools.partial(_norm_proj_kernel, splits=splits),
        grid=(t // tm,),
        in_specs=[
            pl.BlockSpec((tm, d), lambda i: (i, 0)),
            pl.BlockSpec((1, d), lambda i: (0, 0)),
            pl.BlockSpec((d, n), lambda i: (0, 0)),
        ],
        out_specs=[pl.BlockSpec((tm, s), lambda i: (i, 0)) for s in splits],
        out_shape=[jax.ShapeDtypeStruct((t, s), dt) for s, dt in zip(splits, dtypes)],
        compiler_params=_params(("arbitrary",)),
        name=name,
    )(x, g.reshape(1, d), w)


DSA_SPLITS = (ATT_W, ATT_W, ATT_W, IDX_HEADS * IDX_HD, LANES, LANES, MEM_W)


def _rope(xh, c, s_lo, s_hi, half):
    return xh * c + pltpu.roll(xh, LANES - half, 1) * s_lo + pltpu.roll(xh, half, 1) * s_hi


def _dsa_proj_kernel(x_ref, g_ref, w_ref, pos_ref, inva_ref, invi_ref,
                     q_ref, k_ref, v_ref, iq_ref, ik_ref, iw_ref, mq_ref):
    hn = _rms(x_ref[...], g_ref[...]).astype(BF16)
    z = jnp.dot(hn, w_ref[...], preferred_element_type=F32)
    tm = z.shape[0]
    pos = pos_ref[...]
    lane = lax.broadcasted_iota(I32, (tm, LANES), 1)
    ang_a = pos * inva_ref[...]
    ca, sa = jnp.cos(ang_a), jnp.sin(ang_a)
    ha = ATT_ROT // 2
    sa_lo = jnp.where(lane < ha, -sa, 0.0)
    sa_hi = jnp.where((lane >= ha) & (lane < 2 * ha), sa, 0.0)
    ang_i = pos * invi_ref[...]
    ci, si = jnp.cos(ang_i), jnp.sin(ang_i)
    hi = IDX_ROT // 2
    m64 = lane & (IDX_HD - 1)
    si_lo = jnp.where(m64 < hi, -si, 0.0)
    si_hi = jnp.where((m64 >= hi) & (m64 < 2 * hi), si, 0.0)

    off = 0
    for h in range(ATT_HEADS):
        sl = slice(off + h * ATT_HD, off + (h + 1) * ATT_HD)
        qh = _rope(z[:, sl], ca, sa_lo, sa_hi, ha) * (ATT_HD ** -0.5 * LOG2_E)
        q_ref[:, h * ATT_HD:(h + 1) * ATT_HD] = qh.astype(q_ref.dtype)
    off += ATT_W
    for h in range(ATT_HEADS):
        sl = slice(off + h * ATT_HD, off + (h + 1) * ATT_HD)
        k_ref[:, h * ATT_HD:(h + 1) * ATT_HD] = _rope(z[:, sl], ca, sa_lo, sa_hi, ha).astype(k_ref.dtype)
    off += ATT_W
    for j in range(tm // K_CHUNK):
        v_ref[j] = z[j * K_CHUNK:(j + 1) * K_CHUNK, off:off + ATT_W].T.astype(v_ref.dtype)
    off += ATT_W
    for p in range(IDX_HEADS * IDX_HD // LANES):
        sl = slice(off + p * LANES, off + (p + 1) * LANES)
        iq_ref[:, p * LANES:(p + 1) * LANES] = _rope(z[:, sl], ci, si_lo, si_hi, hi).astype(iq_ref.dtype)
    off += IDX_HEADS * IDX_HD
    ik_ref[...] = _rope(z[:, off:off + LANES], ci, si_lo, si_hi, hi).astype(ik_ref.dtype)
    off += LANES
    iw_ref[...] = z[:, off:off + LANES].astype(iw_ref.dtype)
    off += LANES
    mq_ref[...] = z[:, off:off + MEM_W].astype(mq_ref.dtype)


def _dsa_proj(x, g, w, pos, inva, invi):
    t, d = x.shape
    n = w.shape[1]
    tm = TOK_TILE
    dtypes = (BF16, BF16, BF16, BF16, BF16, F32, BF16)
    return pl.pallas_call(
        _dsa_proj_kernel,
        grid=(t // tm,),
        in_specs=[
            pl.BlockSpec((tm, d), lambda i: (i, 0)),
            pl.BlockSpec((1, d), lambda i: (0, 0)),
            pl.BlockSpec((d, n), lambda i: (0, 0)),
            pl.BlockSpec((tm, 1), lambda i: (i, 0)),
            pl.BlockSpec((1, LANES), lambda i: (0, 0)),
            pl.BlockSpec((1, LANES), lambda i: (0, 0)),
        ],
        out_specs=[pl.BlockSpec((tm // K_CHUNK, ATT_W, K_CHUNK), lambda i: (i, 0, 0)) if j == 2
                   else pl.BlockSpec((tm, s), lambda i: (i, 0)) for j, s in enumerate(DSA_SPLITS)],
        out_shape=[jax.ShapeDtypeStruct((t // K_CHUNK, ATT_W, K_CHUNK), dt) if j == 2
                   else jax.ShapeDtypeStruct((t, s), dt) for j, (s, dt) in enumerate(zip(DSA_SPLITS, dtypes))],
        compiler_params=_params(("arbitrary",)),
        name="dsa_proj",
    )(x, g.reshape(1, d), w, pos, inva, invi)


def _lru_kernel(xb_ref, gb_ref, cw_ref, cb_ref, wa_ref, ba_ref, wx_ref, bx_ref, lam_ref, y_ref,
                prev_ref, hc_ref, a_s, b_s):
    ts, c = xb_ref.shape

    @pl.when(pl.program_id(1) == 0)
    def _():
        prev_ref[...] = jnp.zeros_like(prev_ref)
        hc_ref[...] = jnp.zeros_like(hc_ref)

    x = xb_ref[...]
    prev = prev_ref[...]
    xc = cb_ref[...] + cw_ref[3:4, :] * x
    for j in range(1, LRU_CONV):
        xc = xc + cw_ref[LRU_CONV - 1 - j:LRU_CONV - j, :] * _shift_rows(x, prev, j)
    prev_ref[...] = x[ts - SUBLANES:ts, :]

    xcb = xc.astype(BF16)
    ga, gx = [], []
    for t in range(c // MXU_TILE):
        blk = xcb[:, t * MXU_TILE:(t + 1) * MXU_TILE]
        ga.append(jnp.dot(blk, wa_ref[t], preferred_element_type=F32))
        gx.append(jnp.dot(blk, wx_ref[t], preferred_element_type=F32))
    r = jax.nn.sigmoid(jnp.concatenate(ga, axis=1) + ba_ref[...])
    i = jax.nn.sigmoid(jnp.concatenate(gx, axis=1) + bx_ref[...])

    nl = -lam_ref[...]
    softplus = jnp.maximum(nl, 0.0) + jnp.log1p(jnp.exp(-jnp.abs(nl)))
    log_a = (-LRU_C) * r * softplus
    a = jnp.exp(log_a)
    gain = jnp.sqrt(-jnp.tanh(log_a) * (a * a + 1.0))
    bt = gain * (i * xc)

    a3 = a.reshape(ts // SUBLANES, SUBLANES, c)
    b3 = bt.reshape(ts // SUBLANES, SUBLANES, c)
    rid = lax.broadcasted_iota(I32, (1, SUBLANES, c), 1)
    for d in (1, 2, 4):
        a_sh = jnp.where(rid >= d, pltpu.roll(a3, d, 1), 1.0)
        b_sh = jnp.where(rid >= d, pltpu.roll(b3, d, 1), 0.0)
        b3 = a3 * b_sh + b3
        a3 = a3 * a_sh
    a_s[...] = a3.reshape(ts, c)
    b_s[...] = b3.reshape(ts, c)

    def body(g, hc):
        r0 = pl.multiple_of(g * SUBLANES, SUBLANES)
        h = a_s[pl.ds(r0, SUBLANES), :] * hc + b_s[pl.ds(r0, SUBLANES), :]
        b_s[pl.ds(r0, SUBLANES), :] = h
        return jnp.broadcast_to(h[SUBLANES - 1:SUBLANES, :], (SUBLANES, c))

    hc_ref[...] = lax.fori_loop(0, ts // SUBLANES, body, hc_ref[...])
    y_ref[...] = (b_s[...] * jax.nn.gelu(gb_ref[...])).astype(y_ref.dtype)


def _lru_core(xb, gb, cw, cb, wa, ba, wx, bx, lam, batch, seq):
    c = LRU_W
    ts = LRU_TILE
    nt = seq // ts
    row = lambda b, j: (b * nt + j, 0)
    const2 = lambda b, j: (0, 0)
    const3 = lambda b, j: (0, 0, 0)
    return pl.pallas_call(
        _lru_kernel,
        grid=(batch, nt),
        in_specs=[
            pl.BlockSpec((ts, c), row),
            pl.BlockSpec((ts, c), row),
            pl.BlockSpec((LRU_CONV, c), const2),
            pl.BlockSpec((1, c), const2),
            pl.BlockSpec((c // MXU_TILE, MXU_TILE, MXU_TILE), const3),
            pl.BlockSpec((1, c), const2),
            pl.BlockSpec((c // MXU_TILE, MXU_TILE, MXU_TILE), const3),
            pl.BlockSpec((1, c), const2),
            pl.BlockSpec((1, c), const2),
        ],
        out_specs=pl.BlockSpec((ts, c), row),
        out_shape=jax.ShapeDtypeStruct((batch * seq, c), BF16),
        scratch_shapes=[
            pltpu.VMEM((SUBLANES, c), F32),
            pltpu.VMEM((SUBLANES, c), F32),
            pltpu.VMEM((ts, c), F32),
            pltpu.VMEM((ts, c), F32),
        ],
        compiler_params=_params(("arbitrary", "arbitrary")),
        name="lru_core",
    )(xb, gb, cw, cb.reshape(1, c), wa, ba.reshape(1, c), wx, bx.reshape(1, c), lam.reshape(1, c))


def _block_diag_tiles(w):
    per = MXU_TILE // LRU_BW
    w4 = w.reshape(LRU_BLOCKS // per, per, LRU_BW, LRU_BW)
    eye = jnp.eye(per, dtype=w.dtype)
    t = jnp.einsum('gpij,pq->gpiqj', w4, eye)
    return t.reshape(LRU_BLOCKS // per, MXU_TILE, MXU_TILE)


def _mix_out_kernel(a_ref, mq_ref, mk_ref, mv_ref, w_ref, x_ref, o_ref):
    mq = mq_ref[...]
    mk = mk_ref[...]
    mv = mv_ref[...]
    scale = MEM_HD ** -0.5
    n_a = a_ref.shape[1]
    scores = []
    for h in range(MEM_HEADS):
        sl = slice(h * MEM_HD, (h + 1) * MEM_HD)
        scores.append(lax.dot_general(mq[:, sl], mk[:, sl], (((1,), (1,)), ((), ())),
                                      preferred_element_type=F32))
    acc = jnp.dot(a_ref[...], w_ref[0:n_a, :], preferred_element_type=F32)
    heads = []
    for h in range(MEM_HEADS):
        sl = slice(h * MEM_HD, (h + 1) * MEM_HD)
        s = scores[h] * scale
        s = s - jnp.max(s, axis=-1, keepdims=True)
        e = jnp.exp(s)
        p = e / jnp.sum(e, axis=-1, keepdims=True)
        heads.append(jnp.dot(p.astype(BF16), mv[:, sl], preferred_element_type=F32).astype(BF16))
    acc = acc + jnp.dot(jnp.concatenate(heads, axis=1), w_ref[n_a:, :], preferred_element_type=F32)
    o_ref[...] = x_ref[...] + acc


def _mix_out(a, mq, memkv, layer, w_out, x, seq):
    t, d = x.shape
    tm = TOK_TILE
    per = seq // tm
    kin = w_out.shape[0]
    return pl.pallas_call(
        _mix_out_kernel,
        grid=(t // tm,),
        in_specs=[
            pl.BlockSpec((tm, a.shape[1]), lambda i: (i, 0)),
            pl.BlockSpec((tm, MEM_W), lambda i: (i, 0)),
            pl.BlockSpec((MEM_TOKENS, MEM_W), lambda i: (i // per, 2 * layer)),
            pl.BlockSpec((MEM_TOKENS, MEM_W), lambda i: (i // per, 2 * layer + 1)),
            pl.BlockSpec((kin, d), lambda i: (0, 0)),
            pl.BlockSpec((tm, d), lambda i: (i, 0)),
        ],
        out_specs=pl.BlockSpec((tm, d), lambda i: (i, 0)),
        out_shape=jax.ShapeDtypeStruct((t, d), F32),
        compiler_params=_params(("arbitrary",)),
        name="mix_out",
    )(a, mq, memkv, memkv, w_out, x)


def _ffn_kernel(x_ref, g_ref, wup_ref, cw_ref, cb_ref, wdn_ref, fg_ref, o_ref, hn_s, acc_s, uprev_s,
                *, final_norm):
    tm = x_ref.shape[0]
    nch = wdn_ref.shape[0] // FF_CHUNK

    @pl.when(pl.program_id(1) == 0)
    def _():
        uprev_s[...] = jnp.zeros_like(uprev_s)

    hn_s[...] = _rms(x_ref[...], g_ref[...]).astype(BF16)
    acc_s[...] = jnp.zeros_like(acc_s)

    def cols(idx):
        return slice(idx * FF_CHUNK, (idx + 1) * FF_CHUNK)

    def up(idx):
        return jnp.dot(hn_s[...], wup_ref[:, cols(idx)], preferred_element_type=F32)

    def conv(u, idx):
        prev = uprev_s[idx]
        y = cb_ref[:, cols(idx)] + cw_ref[2:3, cols(idx)] * u
        y = y + cw_ref[1:2, cols(idx)] * _shift_rows(u, prev, 1)
        y = y + cw_ref[0:1, cols(idx)] * _shift_rows(u, prev, 2)
        uprev_s[idx] = u[tm - SUBLANES:tm, :]
        return y

    def down(act, c):
        acc_s[...] += jnp.dot(act, wdn_ref[c * FF_CHUNK:(c + 1) * FF_CHUNK, :], preferred_element_type=F32)

    ug, uv = up(0), up(nch)
    act_prev = None
    for c in range(nch):
        if c + 1 < nch:
            ug_next, uv_next = up(c + 1), up(c + 1 + nch)
        if act_prev is not None:
            down(act_prev, c - 1)
        act_prev = (jax.nn.silu(conv(ug, c)) * conv(uv, c + nch)).astype(BF16)
        if c + 1 < nch:
            ug, uv = ug_next, uv_next
    down(act_prev, nch - 1)
    out = x_ref[...] + acc_s[...]
    if final_norm:
        out = _rms(out, fg_ref[...])
    o_ref[...] = out


def _ffn(x, g, w_up, conv_w, conv_b, w_down, layer, fg, batch, seq, final_norm):
    t, d = x.shape
    tm = FFN_TILE
    nt = seq // tm
    n2 = w_up.shape[2]
    row = lambda b, j: (b * nt + j, 0)
    const = lambda b, j: (0, 0)
    pick = lambda b, j: (layer, 0, 0)
    return pl.pallas_call(
        functools.partial(_ffn_kernel, final_norm=final_norm),
        grid=(batch, nt),
        in_specs=[
            pl.BlockSpec((tm, d), row),
            pl.BlockSpec((1, d), const),
            pl.BlockSpec((None,) + w_up.shape[1:], pick),
            pl.BlockSpec((None,) + conv_w.shape[1:], pick),
            pl.BlockSpec((None, 1, n2), pick),
            pl.BlockSpec((None,) + w_down.shape[1:], pick),
            pl.BlockSpec((1, d), const),
        ],
        out_specs=pl.BlockSpec((tm, d), row),
        out_shape=jax.ShapeDtypeStruct((t, d), F32),
        scratch_shapes=[
            pltpu.VMEM((tm, d), BF16),
            pltpu.VMEM((tm, d), F32),
            pltpu.VMEM((n2 // FF_CHUNK, SUBLANES, FF_CHUNK), F32),
        ],
        compiler_params=_params(("arbitrary", "arbitrary")),
        name="ffn",
    )(x, g.reshape(1, d), w_up, conv_w, conv_b.reshape(conv_b.shape[0], 1, n2), w_down, fg.reshape(1, d))


def _dsa_attn_kernel(ik_ref, iq_ref, iw_ref, q_ref, k_ref, vt_ref, o_ref,
                     keys_s, m_s, l_s, acc_s, p_s, alpha_s):
    tq = q_ref.shape[0]
    kc_n = K_CHUNK
    qi = pl.program_id(1)
    nk = qi + 1
    idx_bits = int(keys_s.shape[0]).bit_length()

    iw_t = iw_ref[...].T[0:IDX_HEADS, :] * ((IDX_HEADS ** -0.5) * (IDX_HD ** -0.5))
    iq = iq_ref[...]
    lane = lax.broadcasted_iota(I32, (kc_n, LANES), 1)
    krow = lax.broadcasted_iota(I32, (kc_n, tq), 0)
    qcol = lax.broadcasted_iota(I32, (kc_n, tq), 1)

    def score_body(kc, carry):
        r0 = pl.multiple_of(kc * kc_n, kc_n)
        ik2 = ik_ref[pl.ds(r0, kc_n), :]
        ik_lo = jnp.where(lane < IDX_HD, ik2, jnp.zeros_like(ik2))
        ik_hi = jnp.where(lane >= IDX_HD, ik2, jnp.zeros_like(ik2))
        score = jnp.zeros((kc_n, tq), F32)
        for h in range(IDX_HEADS):
            pair = iq[:, (h // 2) * LANES:(h // 2 + 1) * LANES]
            lhs = ik_lo if h % 2 == 0 else ik_hi
            rel = lax.dot_general(lhs, pair, (((1,), (1,)), ((), ())), preferred_element_type=F32)
            score = score + iw_t[h:h + 1, :] * jnp.maximum(rel, 0.0)
        causal = (krow + r0) <= (qcol + qi * tq)
        keys_s[pl.ds(r0, kc_n), :] = jnp.where(causal, score, -jnp.inf)
        return carry

    lax.fori_loop(0, nk, score_body, 0)

    def count(pred_fn):
        def body(kc, part):
            r0 = pl.multiple_of(kc * kc_n, kc_n)
            ones = jnp.where(pred_fn(keys_s[pl.ds(r0, kc_n), :], r0), 1, 0).astype(I32)
            return part + jnp.sum(ones.reshape(kc_n // COUNT_ACC_ROWS, COUNT_ACC_ROWS, tq), axis=0)
        part = lax.fori_loop(0, nk, body, jnp.zeros((COUNT_ACC_ROWS, tq), I32))
        return jnp.sum(part, axis=0, keepdims=True)

    def ordered_to_f32(u):
        neg_inf_u = jnp.int32(0x007FFFFF)
        u = jnp.where((u >= 0) & (u < neg_inf_u), neg_inf_u, u)
        k = u ^ INT_MIN
        return pltpu.bitcast(k ^ ((k >> 31) & jnp.int32(0x7FFFFFFF)), F32)

    def bit_body(t, carry):
        tu, cnt_ge = carry
        cand_u = tu | lax.shift_left(jnp.int32(1), jnp.int32(31) - t)
        cand_f = ordered_to_f32(cand_u)
        cnt = count(lambda kv, r0: kv >= cand_f)
        ok = cnt >= TOPK_MAX
        return jnp.where(ok, cand_u, tu), jnp.where(ok, cnt, cnt_ge)

    zeros = jnp.zeros((1, tq), I32)
    tu, cnt_ge = lax.fori_loop(0, 32, bit_body, (zeros, zeros + nk * kc_n))
    thr = ordered_to_f32(tu)
    short = thr == -jnp.inf
    excess = jnp.where(short, 0, cnt_ge - TOPK_MAX)

    def tie_search():
        need = TOPK_MAX - count(lambda kv, r0: kv > thr)

        def jbit(t, jc):
            cand = jc | lax.shift_left(jnp.int32(1), jnp.int32(idx_bits - 1) - t)
            cnt = count(lambda kv, r0: (kv == thr) & ((krow + r0) < cand))
            return jnp.where(cnt <= need, cand, jc)
        return lax.fori_loop(0, idx_bits, jbit, jnp.zeros((1, tq), I32))

    jcut = lax.cond(jnp.max(excess) > 0, tie_search, lambda: jnp.full((1, tq), (1 << idx_bits) - 1, I32))

    def bias_body(kc, carry):
        r0 = pl.multiple_of(kc * kc_n, kc_n)
        kv = keys_s[pl.ds(r0, kc_n), :]
        sel = ((kv > thr) | ((kv == thr) & ((krow + r0) < jcut))) & (kv > -jnp.inf)
        keys_s[pl.ds(r0, kc_n), :] = jnp.where(sel, 0.0, NEG_BIG).astype(F32)
        return carry

    lax.fori_loop(0, nk, bias_body, 0)

    m_s[...] = jnp.full_like(m_s, NEG_BIG)
    l_s[...] = jnp.zeros_like(l_s)
    acc_s[...] = jnp.zeros_like(acc_s)

    def softmax_stage(kc):
        slot = kc & 1
        r0 = pl.multiple_of(kc * kc_n, kc_n)
        bias = keys_s[pl.ds(r0, kc_n), :]
        m_all = m_s[...]
        l_all = l_s[...]
        m_rows, l_rows, a_rows = [], [], []
        for h in range(ATT_HEADS):
            sl = slice(h * ATT_HD, (h + 1) * ATT_HD)
            s = lax.dot_general(k_ref[pl.ds(r0, kc_n), sl], q_ref[:, sl], (((1,), (1,)), ((), ())),
                                preferred_element_type=F32) + bias
            m_prev = m_all[h:h + 1, :]
            m_new = jnp.maximum(m_prev, jnp.max(s, axis=0, keepdims=True))
            alpha = jnp.exp2(m_prev - m_new)
            p = jnp.exp2(s - m_new)
            l_rows.append(alpha * l_all[h:h + 1, :] + jnp.sum(p, axis=0, keepdims=True))
            m_rows.append(m_new)
            a_rows.append(alpha)
            p_s[slot, h * kc_n:(h + 1) * kc_n, :] = p.astype(BF16)
        m_s[...] = jnp.concatenate(m_rows, axis=0)
        l_s[...] = jnp.concatenate(l_rows, axis=0)
        alpha_s[slot] = jnp.concatenate(a_rows, axis=0)

    def pv_stage(kc):
        slot = kc & 1
        al = alpha_s[slot]
        for h in range(ATT_HEADS):
            sl = slice(h * ATT_HD, (h + 1) * ATT_HD)
            pv = jnp.dot(vt_ref[kc, sl, :], p_s[slot, h * kc_n:(h + 1) * kc_n, :],
                         preferred_element_type=F32)
            acc_s[sl, :] = al[h:h + 1, :] * acc_s[sl, :] + pv

    def att_body(kc, carry):
        pv_stage(kc - 1)
        softmax_stage(kc)
        return carry

    softmax_stage(jnp.int32(0))
    lax.fori_loop(1, nk, att_body, 0)
    pv_stage(nk - 1)
    for h in range(ATT_HEADS):
        sl = slice(h * ATT_HD, (h + 1) * ATT_HD)
        o_ref[:, sl] = (acc_s[sl, :] / l_s[h:h + 1, :]).T.astype(o_ref.dtype)


def _dsa_attn(ik2, iq, iw, q, k, vt, batch, seq):
    tq = Q_TILE
    assert Q_TILE == K_CHUNK and seq % Q_TILE == 0
    nq = seq // tq
    nkc = seq // K_CHUNK
    r3 = lambda a: a.reshape(batch, seq, a.shape[-1])
    tile = lambda b, j: (b, j, 0)
    full = lambda b, j: (b, 0, 0)
    out = pl.pallas_call(
        _dsa_attn_kernel,
        grid=(batch, nq),
        in_specs=[
            pl.BlockSpec((None, seq, LANES), full),
            pl.BlockSpec((None, tq, IDX_HEADS * IDX_HD), tile),
            pl.BlockSpec((None, tq, LANES), tile),
            pl.BlockSpec((None, tq, ATT_W), tile),
            pl.BlockSpec((None, seq, ATT_W), full),
            pl.BlockSpec((None, nkc, ATT_W, K_CHUNK), lambda b, j: (b, 0, 0, 0)),
        ],
        out_specs=pl.BlockSpec((None, tq, ATT_W), tile),
        out_shape=jax.ShapeDtypeStruct((batch, seq, ATT_W), BF16),
        scratch_shapes=[
            pltpu.VMEM((seq, tq), F32),
            pltpu.VMEM((ATT_HEADS, tq), F32),
            pltpu.VMEM((ATT_HEADS, tq), F32),
            pltpu.VMEM((ATT_W, tq), F32),
            pltpu.VMEM((2, ATT_HEADS * K_CHUNK, tq), BF16),
            pltpu.VMEM((2, ATT_HEADS, tq), F32),
        ],
        compiler_params=_params(("arbitrary", "arbitrary")),
        name="dsa_attn",
    )(r3(ik2), r3(iq), r3(iw), r3(q), r3(k), vt.reshape(batch, nkc, ATT_W, K_CHUNK))
    return out.reshape(batch * seq, ATT_W)


def _rope_inv_rows():
    inv_a = ROPE_THETA ** (-jnp.arange(0, ATT_ROT, 2, dtype=F32) / ATT_ROT)
    inv_i = ROPE_THETA ** (-jnp.arange(0, IDX_ROT, 2, dtype=F32) / IDX_ROT)
    row_a = jnp.concatenate([inv_a, inv_a, jnp.zeros((ATT_HD - ATT_ROT,), F32)])
    half = jnp.concatenate([inv_i, inv_i, jnp.zeros((IDX_HD - IDX_ROT,), F32)])
    row_i = jnp.concatenate([half, half])
    return row_a.reshape(1, LANES), row_i.reshape(1, LANES)


def kernel(x, mem, positions, norm_mix, norm_ffn, mem_norm, final_norm, w_mem_kv, w_ffn_up, ffn_conv_w,
           ffn_conv_b, w_ffn_down, lru_w_in, lru_conv_w, lru_conv_b, lru_w_a, lru_b_a, lru_w_x, lru_b_x,
           lru_lambda, lru_w_out, dsa_w_in, dsa_w_out):
    batch, seq, d = x.shape
    t = batch * seq
    xf = x.reshape(t, d)

    w_kv = jnp.concatenate([w_mem_kv[0], w_mem_kv[1]], axis=1).astype(BF16)
    (memkv,) = _norm_proj(mem.reshape(batch * MEM_TOKENS, d), mem_norm, w_kv,
                          (w_kv.shape[1],), (BF16,), "mem_kv")

    xb, gb, mq = _norm_proj(xf, norm_mix[0], lru_w_in[0].astype(BF16),
                            (LRU_W, LRU_W, MEM_W), (F32, F32, BF16), "lru_proj")
    y = _lru_core(xb, gb, lru_conv_w[0], lru_conv_b[0],
                  _block_diag_tiles(lru_w_a[0]).astype(BF16), lru_b_a[0],
                  _block_diag_tiles(lru_w_x[0]).astype(BF16), lru_b_x[0], lru_lambda[0], batch, seq)
    xf = _mix_out(y, mq, memkv, 0, lru_w_out[0].astype(BF16), xf, seq)
    w_up_bf, w_down_bf = w_ffn_up.astype(BF16), w_ffn_down.astype(BF16)
    xf = _ffn(xf, norm_ffn[0], w_up_bf, ffn_conv_w, ffn_conv_b, w_down_bf, 0, final_norm, batch, seq, False)

    w = dsa_w_in[0].astype(BF16)
    o = np.cumsum((0, ATT_W, ATT_W, ATT_W, IDX_HEADS * IDX_HD, IDX_HD, IDX_HEADS, MEM_W))
    w_ik = w[:, o[4]:o[5]]
    w_iw = jnp.pad(w[:, o[5]:o[6]], ((0, 0), (0, LANES - IDX_HEADS)))
    w_cat = jnp.concatenate([w[:, :o[4]], w_ik, w_ik, w_iw, w[:, o[6]:o[7]]], axis=1)
    inva, invi = _rope_inv_rows()
    pos = positions.astype(F32).reshape(t, 1)
    q, k, vt, iq, ik2, iw, mq = _dsa_proj(xf, norm_mix[1], w_cat, pos, inva, invi)
    att = _dsa_attn(ik2, iq, iw, q, k, vt, batch, seq)
    xf = _mix_out(att, mq, memkv, 1, dsa_w_out[0].astype(BF16), xf, seq)
    xf = _ffn(xf, norm_ffn[1], w_up_bf, ffn_conv_w, ffn_conv_b, w_down_bf, 1, final_norm, batch, seq, True)
    return xf.reshape(batch, seq, d)
```

```python
import functools

import numpy as np
import jax
import jax.numpy as jnp
from jax import lax
from jax.experimental import pallas as pl
from jax.experimental.pallas import tpu as pltpu

F32 = jnp.float32
BF16 = jnp.bfloat16
I32 = jnp.int32

D_MODEL = 1024
RMS_EPS = 1e-6
ROPE_THETA = 500000.0

LRU_W = 1024
LRU_BLOCKS = 16
LRU_BW = LRU_W // LRU_BLOCKS
LRU_CONV = 4
LRU_C = 8.0

ATT_HEADS = 8
ATT_HD = 128
ATT_W = ATT_HEADS * ATT_HD
ATT_ROT = ATT_HD // 4
IDX_HEADS = 8
IDX_HD = 64
IDX_ROT = IDX_HD // 4
TOPK_MAX = 256

MEM_TOKENS = 256
MEM_HEADS = 4
MEM_HD = 128
MEM_W = MEM_HEADS * MEM_HD

D_FF = 2816
FFN_CONV = 3

SUBLANES = 8
LANES = 128
MXU_TILE = 256
VMEM_LIMIT_BYTES = 56 * 1024 * 1024

INT_MIN = np.int32(-2 ** 31)
NEG_BIG = -1e30
LOG2_E = 1.4426950408889634

TOK_TILE = 512
FFN_TILE = 512
LRU_TILE = 512
Q_TILE = 256
K_CHUNK = 256
FF_CHUNK = 256
COUNT_ACC_ROWS = 16


def _params(sem):
    return pltpu.CompilerParams(dimension_semantics=sem, vmem_limit_bytes=VMEM_LIMIT_BYTES)


def _rms(x, g):
    ms = jnp.mean(x * x, axis=-1, keepdims=True)
    return x * lax.rsqrt(ms + RMS_EPS) * g


def _shift_rows(x, prev, j):
    r = pltpu.roll(x, j, 0)
    p = pltpu.roll(prev, j, 0)
    rid = lax.broadcasted_iota(I32, (SUBLANES, x.shape[1]), 0)
    top = jnp.where(rid < j, p, r[0:SUBLANES])
    return jnp.concatenate([top, r[SUBLANES:]], axis=0)


def _norm_proj_kernel(x_ref, g_ref, w_ref, *out_refs, splits):
    hn = _rms(x_ref[...], g_ref[...]).astype(BF16)
    z = jnp.dot(hn, w_ref[...], preferred_element_type=F32)
    off = 0
    for o_ref, n in zip(out_refs, splits):
        o_ref[...] = z[:, off:off + n].astype(o_ref.dtype)
        off += n


def _norm_proj(x, g, w, splits, dtypes, name):
    t, d = x.shape
    n = w.shape[1]
    tm = min(TOK_TILE, t)
    return pl.pallas_call(
        functools.partial(_norm_proj_kernel, splits=splits),
        grid=(t // tm,),
        in_specs=[
            pl.BlockSpec((tm, d), lambda i: (i, 0)),
            pl.BlockSpec((1, d), lambda i: (0, 0)),
            pl.BlockSpec((d, n), lambda i: (0, 0)),
        ],
        out_specs=[pl.BlockSpec((tm, s), lambda i: (i, 0)) for s in splits],
        out_shape=[jax.ShapeDtypeStruct((t, s), dt) for s, dt in zip(splits, dtypes)],
        compiler_params=_params(("arbitrary",)),
        name=name,
    )(x, g.reshape(1, d), w)


DSA_SPLITS = (ATT_W, ATT_W, ATT_W, IDX_HEADS * IDX_HD, LANES, LANES, MEM_W)
ROPE_LANES = ATT_ROT
ROPE_PACK = LANES // ROPE_LANES


def _rope(xh, c, s_lo, s_hi, half):
    return xh * c + pltpu.roll(xh, LANES - half, 1) * s_lo + pltpu.roll(xh, half, 1) * s_hi


def _dsa_proj_kernel(x_ref, g_ref, w_ref, pos_ref, inva_ref, invi_ref,
                     q_ref, k_ref, v_ref, iq_ref, ik_ref, iw_ref, mq_ref):
    hn = _rms(x_ref[...], g_ref[...]).astype(BF16)
    z = jnp.dot(hn, w_ref[...], preferred_element_type=F32)
    tm = z.shape[0]
    pos = pos_ref[...]
    rows = tm // ROPE_PACK
    lane = lax.broadcasted_iota(I32, (rows, LANES), 1)
    pp = jnp.broadcast_to(pos[(ROPE_PACK - 1) * rows:, :], (rows, LANES))
    for j in range(ROPE_PACK - 2, -1, -1):
        pp = jnp.where(lane < (j + 1) * ROPE_LANES,
                       jnp.broadcast_to(pos[j * rows:(j + 1) * rows, :], (rows, LANES)), pp)
    ang_a = pp * inva_ref[...]
    ang_i = pp * invi_ref[...]
    cos_a, sin_a, cos_i, sin_i = jnp.cos(ang_a), jnp.sin(ang_a), jnp.cos(ang_i), jnp.sin(ang_i)

    def unpack(tab, j):
        return tab if j == 0 else pltpu.roll(tab, LANES - j * ROPE_LANES, 1)

    ha = ATT_ROT // 2
    hi = IDX_ROT // 2
    m64 = lane & (IDX_HD - 1)
    in_a = lane < ROPE_LANES
    in_i = m64 < ROPE_LANES
    ca, sa_lo, sa_hi, ci, si_lo, si_hi = [], [], [], [], [], []
    for j in range(ROPE_PACK):
        c, s = unpack(cos_a, j), unpack(sin_a, j)
        ca.append(jnp.where(in_a, c, 1.0))
        sa_lo.append(jnp.where(lane < ha, -s, 0.0))
        sa_hi.append(jnp.where((lane >= ha) & (lane < 2 * ha), s, 0.0))
        c, s = unpack(cos_i, j), unpack(sin_i, j)
        c = jnp.where(lane < IDX_HD, c, pltpu.roll(c, IDX_HD, 1))
        s = jnp.where(lane < IDX_HD, s, pltpu.roll(s, IDX_HD, 1))
        ci.append(jnp.where(in_i, c, 1.0))
        si_lo.append(jnp.where(m64 < hi, -s, 0.0))
        si_hi.append(jnp.where((m64 >= hi) & (m64 < 2 * hi), s, 0.0))
    ca, sa_lo, sa_hi, ci, si_lo, si_hi = [jnp.concatenate(t, axis=0)
                                          for t in (ca, sa_lo, sa_hi, ci, si_lo, si_hi)]

    off = 0
    for h in range(ATT_HEADS):
        sl = slice(off + h * ATT_HD, off + (h + 1) * ATT_HD)
        qh = _rope(z[:, sl], ca, sa_lo, sa_hi, ha) * (ATT_HD ** -0.5 * LOG2_E)
        q_ref[:, h * ATT_HD:(h + 1) * ATT_HD] = qh.astype(q_ref.dtype)
    off += ATT_W
    for h in range(ATT_HEADS):
        sl = slice(off + h * ATT_HD, off + (h + 1) * ATT_HD)
        k_ref[:, h * ATT_HD:(h + 1) * ATT_HD] = _rope(z[:, sl], ca, sa_lo, sa_hi, ha).astype(k_ref.dtype)
    off += ATT_W
    for j in range(tm // K_CHUNK):
        v_ref[j] = z[j * K_CHUNK:(j + 1) * K_CHUNK, off:off + ATT_W].T.astype(v_ref.dtype)
    off += ATT_W
    for p in range(IDX_HEADS * IDX_HD // LANES):
        sl = slice(off + p * LANES, off + (p + 1) * LANES)
        iq_ref[:, p * LANES:(p + 1) * LANES] = _rope(z[:, sl], ci, si_lo, si_hi, hi).astype(iq_ref.dtype)
    off += IDX_HEADS * IDX_HD
    ik_ref[...] = _rope(z[:, off:off + LANES], ci, si_lo, si_hi, hi).astype(ik_ref.dtype)
    off += LANES
    iw_ref[...] = z[:, off:off + LANES].astype(iw_ref.dtype)
    off += LANES
    mq_ref[...] = z[:, off:off + MEM_W].astype(mq_ref.dtype)


def _dsa_proj(x, g, w, pos, inva, invi):
    t, d = x.shape
    n = w.shape[1]
    tm = TOK_TILE
    dtypes = (BF16, BF16, BF16, BF16, BF16, F32, BF16)
    return pl.pallas_call(
        _dsa_proj_kernel,
        grid=(t // tm,),
        in_specs=[
            pl.BlockSpec((tm, d), lambda i: (i, 0)),
            pl.BlockSpec((1, d), lambda i: (0, 0)),
            pl.BlockSpec((d, n), lambda i: (0, 0)),
            pl.BlockSpec((tm, 1), lambda i: (i, 0)),
            pl.BlockSpec((1, LANES), lambda i: (0, 0)),
            pl.BlockSpec((1, LANES), lambda i: (0, 0)),
        ],
        out_specs=[pl.BlockSpec((tm // K_CHUNK, ATT_W, K_CHUNK), lambda i: (i, 0, 0)) if j == 2
                   else pl.BlockSpec((tm, s), lambda i: (i, 0)) for j, s in enumerate(DSA_SPLITS)],
        out_shape=[jax.ShapeDtypeStruct((t // K_CHUNK, ATT_W, K_CHUNK), dt) if j == 2
                   else jax.ShapeDtypeStruct((t, s), dt) for j, (s, dt) in enumerate(zip(DSA_SPLITS, dtypes))],
        compiler_params=_params(("arbitrary",)),
        name="dsa_proj",
    )(x, g.reshape(1, d), w, pos, inva, invi)


def _lru_kernel(xb_ref, gb_ref, cw_ref, cb_ref, wa_ref, ba_ref, wx_ref, bx_ref, lam_ref, y_ref,
                prev_ref, hc_ref, a_s, b_s):
    ts, c = xb_ref.shape

    @pl.when(pl.program_id(1) == 0)
    def _():
        prev_ref[...] = jnp.zeros_like(prev_ref)
        hc_ref[...] = jnp.zeros_like(hc_ref)

    x = xb_ref[...]
    prev = prev_ref[...]
    xc = cb_ref[...] + cw_ref[3:4, :] * x
    for j in range(1, LRU_CONV):
        xc = xc + cw_ref[LRU_CONV - 1 - j:LRU_CONV - j, :] * _shift_rows(x, prev, j)
    prev_ref[...] = x[ts - SUBLANES:ts, :]

    xcb = xc.astype(BF16)
    ga, gx = [], []
    for t in range(c // MXU_TILE):
        blk = xcb[:, t * MXU_TILE:(t + 1) * MXU_TILE]
        ga.append(jnp.dot(blk, wa_ref[t], preferred_element_type=F32))
        gx.append(jnp.dot(blk, wx_ref[t], preferred_element_type=F32))
    r = jax.nn.sigmoid(jnp.concatenate(ga, axis=1) + ba_ref[...])
    i = jax.nn.sigmoid(jnp.concatenate(gx, axis=1) + bx_ref[...])

    nl = -lam_ref[...]
    softplus = jnp.maximum(nl, 0.0) + jnp.log1p(jnp.exp(-jnp.abs(nl)))
    log_a = r * ((-LRU_C) * softplus)
    a = jnp.exp(log_a)
    gain = jnp.sqrt(-jnp.tanh(log_a) * (a * a + 1.0))
    bt = gain * (i * xc)

    a3 = a.reshape(ts // SUBLANES, SUBLANES, c)
    b3 = bt.reshape(ts // SUBLANES, SUBLANES, c)
    rid = lax.broadcasted_iota(I32, (1, SUBLANES, c), 1)
    for d in (1, 2, 4):
        a_sh = jnp.where(rid >= d, pltpu.roll(a3, d, 1), 1.0)
        b_sh = jnp.where(rid >= d, pltpu.roll(b3, d, 1), 0.0)
        b3 = a3 * b_sh + b3
        a3 = a3 * a_sh
    a_s[...] = a3.reshape(ts, c)
    b_s[...] = b3.reshape(ts, c)

    def body(g, hc):
        r0 = pl.multiple_of(g * SUBLANES, SUBLANES)
        h = a_s[pl.ds(r0, SUBLANES), :] * hc + b_s[pl.ds(r0, SUBLANES), :]
        b_s[pl.ds(r0, SUBLANES), :] = h
        return jnp.broadcast_to(h[SUBLANES - 1:SUBLANES, :], (SUBLANES, c))

    hc_ref[...] = lax.fori_loop(0, ts // SUBLANES, body, hc_ref[...])
    y_ref[...] = (b_s[...] * jax.nn.gelu(gb_ref[...])).astype(y_ref.dtype)


def _lru_core(xb, gb, cw, cb, wa, ba, wx, bx, lam, batch, seq):
    c = LRU_W
    ts = LRU_TILE
    nt = seq // ts
    row = lambda b, j: (b * nt + j, 0)
    const2 = lambda b, j: (0, 0)
    const3 = lambda b, j: (0, 0, 0)
    return pl.pallas_call(
        _lru_kernel,
        grid=(batch, nt),
        in_specs=[
            pl.BlockSpec((ts, c), row),
            pl.BlockSpec((ts, c), row),
            pl.BlockSpec((LRU_CONV, c), const2),
            pl.BlockSpec((1, c), const2),
            pl.BlockSpec((c // MXU_TILE, MXU_TILE, MXU_TILE), const3),
            pl.BlockSpec((1, c), const2),
            pl.BlockSpec((c // MXU_TILE, MXU_TILE, MXU_TILE), const3),
            pl.BlockSpec((1, c), const2),
            pl.BlockSpec((1, c), const2),
        ],
        out_specs=pl.BlockSpec((ts, c), row),
        out_shape=jax.ShapeDtypeStruct((batch * seq, c), BF16),
        scratch_shapes=[
            pltpu.VMEM((SUBLANES, c), F32),
            pltpu.VMEM((SUBLANES, c), F32),
            pltpu.VMEM((ts, c), F32),
            pltpu.VMEM((ts, c), F32),
        ],
        compiler_params=_params(("arbitrary", "arbitrary")),
        name="lru_core",
    )(xb, gb, cw, cb.reshape(1, c), wa, ba.reshape(1, c), wx, bx.reshape(1, c), lam.reshape(1, c))


def _block_diag_tiles(w):
    per = MXU_TILE // LRU_BW
    w4 = w.reshape(LRU_BLOCKS // per, per, LRU_BW, LRU_BW)
    eye = jnp.eye(per, dtype=w.dtype)
    t = jnp.einsum('gpij,pq->gpiqj', w4, eye)
    return t.reshape(LRU_BLOCKS // per, MXU_TILE, MXU_TILE)


def _mix_out_kernel(a_ref, mq_ref, mk_ref, mv_ref, w_ref, x_ref, o_ref):
    mq = mq_ref[...]
    mk = mk_ref[...]
    mv = mv_ref[...]
    scale = MEM_HD ** -0.5
    n_a = a_ref.shape[1]
    scores = []
    for h in range(MEM_HEADS):
        sl = slice(h * MEM_HD, (h + 1) * MEM_HD)
        scores.append(lax.dot_general(mq[:, sl], mk[:, sl], (((1,), (1,)), ((), ())),
                                      preferred_element_type=F32))
    acc = jnp.dot(a_ref[...], w_ref[0:n_a, :], preferred_element_type=F32)
    heads = []
    for h in range(MEM_HEADS):
        sl = slice(h * MEM_HD, (h + 1) * MEM_HD)
        s = scores[h] * scale
        s = s - jnp.max(s, axis=-1, keepdims=True)
        e = jnp.exp(s)
        p = e / jnp.sum(e, axis=-1, keepdims=True)
        heads.append(jnp.dot(p.astype(BF16), mv[:, sl], preferred_element_type=F32).astype(BF16))
    acc = acc + jnp.dot(jnp.concatenate(heads, axis=1), w_ref[n_a:, :], preferred_element_type=F32)
    o_ref[...] = x_ref[...] + acc


def _mix_out(a, mq, memkv, layer, w_out, x, seq):
    t, d = x.shape
    tm = TOK_TILE
    per = seq // tm
    kin = w_out.shape[0]
    return pl.pallas_call(
        _mix_out_kernel,
        grid=(t // tm,),
        in_specs=[
            pl.BlockSpec((tm, a.shape[1]), lambda i: (i, 0)),
            pl.BlockSpec((tm, MEM_W), lambda i: (i, 0)),
            pl.BlockSpec((MEM_TOKENS, MEM_W), lambda i: (i // per, 2 * layer)),
            pl.BlockSpec((MEM_TOKENS, MEM_W), lambda i: (i // per, 2 * layer + 1)),
            pl.BlockSpec((kin, d), lambda i: (0, 0)),
            pl.BlockSpec((tm, d), lambda i: (i, 0)),
        ],
        out_specs=pl.BlockSpec((tm, d), lambda i: (i, 0)),
        out_shape=jax.ShapeDtypeStruct((t, d), F32),
        compiler_params=_params(("arbitrary",)),
        name="mix_out",
    )(a, mq, memkv, memkv, w_out, x)


def _ffn_kernel(x_ref, g_ref, wup_ref, cw_ref, cb_ref, wdn_ref, fg_ref, o_ref, hn_s, acc_s, uprev_s,
                *, final_norm):
    tm = x_ref.shape[0]
    nch = wdn_ref.shape[0] // FF_CHUNK

    @pl.when(pl.program_id(1) == 0)
    def _():
        uprev_s[...] = jnp.zeros_like(uprev_s)

    hn_s[...] = _rms(x_ref[...], g_ref[...]).astype(BF16)
    acc_s[...] = jnp.zeros_like(acc_s)

    def cols(idx):
        return slice(idx * FF_CHUNK, (idx + 1) * FF_CHUNK)

    def up(idx):
        return jnp.dot(hn_s[...], wup_ref[:, cols(idx)], preferred_element_type=F32)

    def conv(u, idx):
        prev = uprev_s[idx]
        y = cb_ref[:, cols(idx)] + cw_ref[2:3, cols(idx)] * u
        y = y + cw_ref[1:2, cols(idx)] * _shift_rows(u, prev, 1)
        y = y + cw_ref[0:1, cols(idx)] * _shift_rows(u, prev, 2)
        uprev_s[idx] = u[tm - SUBLANES:tm, :]
        return y

    def down(act, c):
        acc_s[...] += jnp.dot(act, wdn_ref[c * FF_CHUNK:(c + 1) * FF_CHUNK, :], preferred_element_type=F32)

    ug, uv = up(0), up(nch)
    act_prev = None
    for c in range(nch):
        if c + 1 < nch:
            ug_next, uv_next = up(c + 1), up(c + 1 + nch)
        if act_prev is not None:
            down(act_prev, c - 1)
        act_prev = (jax.nn.silu(conv(ug, c)) * conv(uv, c + nch)).astype(BF16)
        if c + 1 < nch:
            ug, uv = ug_next, uv_next
    down(act_prev, nch - 1)
    out = x_ref[...] + acc_s[...]
    if final_norm:
        out = _rms(out, fg_ref[...])
    o_ref[...] = out


def _ffn(x, g, w_up, conv_w, conv_b, w_down, layer, fg, batch, seq, final_norm):
    t, d = x.shape
    tm = FFN_TILE
    nt = seq // tm
    n2 = w_up.shape[2]
    row = lambda b, j: (b * nt + j, 0)
    const = lambda b, j: (0, 0)
    pick = lambda b, j: (layer, 0, 0)
    return pl.pallas_call(
        functools.partial(_ffn_kernel, final_norm=final_norm),
        grid=(batch, nt),
        in_specs=[
            pl.BlockSpec((tm, d), row),
            pl.BlockSpec((1, d), const),
            pl.BlockSpec((None,) + w_up.shape[1:], pick),
            pl.BlockSpec((None,) + conv_w.shape[1:], pick),
            pl.BlockSpec((None, 1, n2), pick),
            pl.BlockSpec((None,) + w_down.shape[1:], pick),
            pl.BlockSpec((1, d), const),
        ],
        out_specs=pl.BlockSpec((tm, d), row),
        out_shape=jax.ShapeDtypeStruct((t, d), F32),
        scratch_shapes=[
            pltpu.VMEM((tm, d), BF16),
            pltpu.VMEM((tm, d), F32),
            pltpu.VMEM((n2 // FF_CHUNK, SUBLANES, FF_CHUNK), F32),
        ],
        compiler_params=_params(("arbitrary", "arbitrary")),
        name="ffn",
    )(x, g.reshape(1, d), w_up, conv_w, conv_b.reshape(conv_b.shape[0], 1, n2), w_down, fg.reshape(1, d))


def _dsa_attn_kernel(ik_ref, iq_ref, iw_ref, q_ref, k_ref, vt_ref, o_ref,
                     keys_s, m_s, l_s, acc_s, p_s, alpha_s):
    tq = q_ref.shape[0]
    kc_n = K_CHUNK
    qi = pl.program_id(1)
    nk = qi + 1
    idx_bits = int(keys_s.shape[0]).bit_length()

    iw_t = iw_ref[...].T[0:IDX_HEADS, :] * ((IDX_HEADS ** -0.5) * (IDX_HD ** -0.5))
    iq = iq_ref[...]
    lane = lax.broadcasted_iota(I32, (kc_n, LANES), 1)
    krow = lax.broadcasted_iota(I32, (kc_n, tq), 0)
    qcol = lax.broadcasted_iota(I32, (kc_n, tq), 1)

    def score_body(kc, carry):
        r0 = pl.multiple_of(kc * kc_n, kc_n)
        ik2 = ik_ref[pl.ds(r0, kc_n), :]
        ik_lo = jnp.where(lane < IDX_HD, ik2, jnp.zeros_like(ik2))
        ik_hi = jnp.where(lane >= IDX_HD, ik2, jnp.zeros_like(ik2))
        score = jnp.zeros((kc_n, tq), F32)
        for h in range(IDX_HEADS):
            pair = iq[:, (h // 2) * LANES:(h // 2 + 1) * LANES]
            lhs = ik_lo if h % 2 == 0 else ik_hi
            rel = lax.dot_general(lhs, pair, (((1,), (1,)), ((), ())), preferred_element_type=F32)
            score = score + iw_t[h:h + 1, :] * jnp.maximum(rel, 0.0)
        causal = (krow + r0) <= (qcol + qi * tq)
        keys_s[pl.ds(r0, kc_n), :] = jnp.where(causal, score, -jnp.inf)
        return carry

    lax.fori_loop(0, nk, score_body, 0)

    def count(pred_fn):
        def body(kc, part):
            r0 = pl.multiple_of(kc * kc_n, kc_n)
            ones = jnp.where(pred_fn(keys_s[pl.ds(r0, kc_n), :], r0), 1, 0).astype(I32)
            return part + jnp.sum(ones.reshape(kc_n // COUNT_ACC_ROWS, COUNT_ACC_ROWS, tq), axis=0)
        part = lax.fori_loop(0, nk, body, jnp.zeros((COUNT_ACC_ROWS, tq), I32))
        return jnp.sum(part, axis=0, keepdims=True)

    def ordered_to_f32(u):
        neg_inf_u = jnp.int32(0x007FFFFF)
        u = jnp.where((u >= 0) & (u < neg_inf_u), neg_inf_u, u)
        k = u ^ INT_MIN
        return pltpu.bitcast(k ^ ((k >> 31) & jnp.int32(0x7FFFFFFF)), F32)

    def bit_body(t, carry):
        tu, cnt_ge = carry
        cand_u = tu | lax.shift_left(jnp.int32(1), jnp.int32(31) - t)
        cand_f = ordered_to_f32(cand_u)
        cnt = count(lambda kv, r0: kv >= cand_f)
        ok = cnt >= TOPK_MAX
        return jnp.where(ok, cand_u, tu), jnp.where(ok, cnt, cnt_ge)

    zeros = jnp.zeros((1, tq), I32)
    tu, cnt_ge = lax.fori_loop(0, 32, bit_body, (zeros, zeros + nk * kc_n))
    thr = ordered_to_f32(tu)
    short = thr == -jnp.inf
    excess = jnp.where(short, 0, cnt_ge - TOPK_MAX)

    def tie_search():
        need = TOPK_MAX - count(lambda kv, r0: kv > thr)

        def jbit(t, jc):
            cand = jc | lax.shift_left(jnp.int32(1), jnp.int32(idx_bits - 1) - t)
            cnt = count(lambda kv, r0: (kv == thr) & ((krow + r0) < cand))
            return jnp.where(cnt <= need, cand, jc)
        return lax.fori_loop(0, idx_bits, jbit, jnp.zeros((1, tq), I32))

    jcut = lax.cond(jnp.max(excess) > 0, tie_search, lambda: jnp.full((1, tq), (1 << idx_bits) - 1, I32))

    def bias_body(kc, carry):
        r0 = pl.multiple_of(kc * kc_n, kc_n)
        kv = keys_s[pl.ds(r0, kc_n), :]
        sel = ((kv > thr) | ((kv == thr) & ((krow + r0) < jcut))) & (kv > -jnp.inf)
        keys_s[pl.ds(r0, kc_n), :] = jnp.where(sel, 0.0, NEG_BIG).astype(F32)
        return carry

    lax.fori_loop(0, nk, bias_body, 0)

    m_s[...] = jnp.full_like(m_s, NEG_BIG)
    l_s[...] = jnp.zeros_like(l_s)
    acc_s[...] = jnp.zeros_like(acc_s)

    def softmax_stage(kc):
        slot = kc & 1
        r0 = pl.multiple_of(kc * kc_n, kc_n)
        bias = keys_s[pl.ds(r0, kc_n), :]
        m_all = m_s[...]
        l_all = l_s[...]
        m_rows, l_rows, a_rows = [], [], []
        for h in range(ATT_HEADS):
            sl = slice(h * ATT_HD, (h + 1) * ATT_HD)
            s = lax.dot_general(k_ref[pl.ds(r0, kc_n), sl], q_ref[:, sl], (((1,), (1,)), ((), ())),
                                preferred_element_type=F32) + bias
            m_prev = m_all[h:h + 1, :]
            m_new = jnp.maximum(m_prev, jnp.max(s, axis=0, keepdims=True))
            alpha = jnp.exp2(m_prev - m_new)
            p = jnp.exp2(s - m_new)
            l_rows.append(alpha * l_all[h:h + 1, :] + jnp.sum(p, axis=0, keepdims=True))
            m_rows.append(m_new)
            a_rows.append(alpha)
            p_s[slot, h * kc_n:(h + 1) * kc_n, :] = p.astype(BF16)
        m_s[...] = jnp.concatenate(m_rows, axis=0)
        l_s[...] = jnp.concatenate(l_rows, axis=0)
        alpha_s[slot] = jnp.concatenate(a_rows, axis=0)

    def pv_stage(kc):
        slot = kc & 1
        al = alpha_s[slot]
        for h in range(ATT_HEADS):
            sl = slice(h * ATT_HD, (h + 1) * ATT_HD)
            pv = jnp.dot(vt_ref[kc, sl, :], p_s[slot, h * kc_n:(h + 1) * kc_n, :],
                         preferred_element_type=F32)
            acc_s[sl, :] = al[h:h + 1, :] * acc_s[sl, :] + pv

    def att_body(kc, carry):
        pv_stage(kc - 1)
        softmax_stage(kc)
        return carry

    softmax_stage(jnp.int32(0))
    lax.fori_loop(1, nk, att_body, 0)
    pv_stage(nk - 1)
    for h in range(ATT_HEADS):
        sl = slice(h * ATT_HD, (h + 1) * ATT_HD)
        o_ref[:, sl] = (acc_s[sl, :] / l_s[h:h + 1, :]).T.astype(o_ref.dtype)


def _dsa_attn(ik2, iq, iw, q, k, vt, batch, seq):
    tq = Q_TILE
    assert Q_TILE == K_CHUNK and seq % Q_TILE == 0
    nq = seq // tq
    nkc = seq // K_CHUNK
    r3 = lambda a: a.reshape(batch, seq, a.shape[-1])
    tile = lambda b, j: (b, j, 0)
    full = lambda b, j: (b, 0, 0)
    out = pl.pallas_call(
        _dsa_attn_kernel,
        grid=(batch, nq),
        in_specs=[
            pl.BlockSpec((None, seq, LANES), full),
            pl.BlockSpec((None, tq, IDX_HEADS * IDX_HD), tile),
            pl.BlockSpec((None, tq, LANES), tile),
            pl.BlockSpec((None, tq, ATT_W), tile),
            pl.BlockSpec((None, seq, ATT_W), full),
            pl.BlockSpec((None, nkc, ATT_W, K_CHUNK), lambda b, j: (b, 0, 0, 0)),
        ],
        out_specs=pl.BlockSpec((None, tq, ATT_W), tile),
        out_shape=jax.ShapeDtypeStruct((batch, seq, ATT_W), BF16),
        scratch_shapes=[
            pltpu.VMEM((seq, tq), F32),
            pltpu.VMEM((ATT_HEADS, tq), F32),
            pltpu.VMEM((ATT_HEADS, tq), F32),
            pltpu.VMEM((ATT_W, tq), F32),
            pltpu.VMEM((2, ATT_HEADS * K_CHUNK, tq), BF16),
            pltpu.VMEM((2, ATT_HEADS, tq), F32),
        ],
        compiler_params=_params(("arbitrary", "arbitrary")),
        name="dsa_attn",
    )(r3(ik2), r3(iq), r3(iw), r3(q), r3(k), vt.reshape(batch, nkc, ATT_W, K_CHUNK))
    return out.reshape(batch * seq, ATT_W)


def _rope_inv_rows():
    inv_a = ROPE_THETA ** (-jnp.arange(0, ATT_ROT, 2, dtype=F32) / ATT_ROT)
    inv_i = ROPE_THETA ** (-jnp.arange(0, IDX_ROT, 2, dtype=F32) / IDX_ROT)
    row_a = jnp.tile(jnp.concatenate([inv_a, inv_a]), ROPE_PACK)
    row_i = jnp.tile(jnp.concatenate([inv_i, inv_i, jnp.zeros((ROPE_LANES - IDX_ROT,), F32)]), ROPE_PACK)
    return row_a.reshape(1, LANES), row_i.reshape(1, LANES)


def kernel(x, mem, positions, norm_mix, norm_ffn, mem_norm, final_norm, w_mem_kv, w_ffn_up, ffn_conv_w,
           ffn_conv_b, w_ffn_down, lru_w_in, lru_conv_w, lru_conv_b, lru_w_a, lru_b_a, lru_w_x, lru_b_x,
           lru_lambda, lru_w_out, dsa_w_in, dsa_w_out):
    batch, seq, d = x.shape
    t = batch * seq
    xf = x.reshape(t, d)

    w_kv = jnp.concatenate([w_mem_kv[0], w_mem_kv[1]], axis=1).astype(BF16)
    (memkv,) = _norm_proj(mem.reshape(batch * MEM_TOKENS, d), mem_norm, w_kv,
                          (w_kv.shape[1],), (BF16,), "mem_kv")

    xb, gb, mq = _norm_proj(xf, norm_mix[0], lru_w_in[0].astype(BF16),
                            (LRU_W, LRU_W, MEM_W), (F32, F32, BF16), "lru_proj")
    y = _lru_core(xb, gb, lru_conv_w[0], lru_conv_b[0],
                  _block_diag_tiles(lru_w_a[0]).astype(BF16), lru_b_a[0],
                  _block_diag_tiles(lru_w_x[0]).astype(BF16), lru_b_x[0], lru_lambda[0], batch, seq)
    xf = _mix_out(y, mq, memkv, 0, lru_w_out[0].astype(BF16), xf, seq)
    w_up_bf, w_down_bf = w_ffn_up.astype(BF16), w_ffn_down.astype(BF16)
    xf = _ffn(xf, norm_ffn[0], w_up_bf, ffn_conv_w, ffn_conv_b, w_down_bf, 0, final_norm, batch, seq, False)

    w = dsa_w_in[0].astype(BF16)
    o = np.cumsum((0, ATT_W, ATT_W, ATT_W, IDX_HEADS * IDX_HD, IDX_HD, IDX_HEADS, MEM_W))
    w_ik = w[:, o[4]:o[5]]
    w_iw = jnp.pad(w[:, o[5]:o[6]], ((0, 0), (0, LANES - IDX_HEADS)))
    w_cat = jnp.concatenate([w[:, :o[4]], w_ik, w_ik, w_iw, w[:, o[6]:o[7]]], axis=1)
    inva, invi = _rope_inv_rows()
    pos = positions.astype(F32).reshape(t, 1)
    q, k, vt, iq, ik2, iw, mq = _dsa_proj(xf, norm_mix[1], w_cat, pos, inva, invi)
    att = _dsa_attn(ik2, iq, iw, q, k, vt, batch, seq)
    xf = _mix_out(att, mq, memkv, 1, dsa_w_out[0].astype(BF16), xf, seq)
    xf = _ffn(xf, norm_ffn[1], w_up_bf, ffn_conv_w, ffn_conv_b, w_down_bf, 1, final_norm, batch, seq, True)
    return xf.reshape(batch, seq, d)
```

```python
import functools

import numpy as np
import jax
import jax.numpy as jnp
from jax import lax
from jax.experimental import pallas as pl
from jax.experimental.pallas import tpu as pltpu

F32 = jnp.float32
BF16 = jnp.bfloat16
I32 = jnp.int32

D_MODEL = 1024
RMS_EPS = 1e-6
ROPE_THETA = 500000.0

LRU_W = 1024
LRU_BLOCKS = 16
LRU_BW = LRU_W // LRU_BLOCKS
LRU_CONV = 4
LRU_C = 8.0

ATT_HEADS = 8
ATT_HD = 128
ATT_W = ATT_HEADS * ATT_HD
ATT_ROT = ATT_HD // 4
IDX_HEADS = 8
IDX_HD = 64
IDX_ROT = IDX_HD // 4
TOPK_MAX = 256

MEM_TOKENS = 256
MEM_HEADS = 4
MEM_HD = 128
MEM_W = MEM_HEADS * MEM_HD

D_FF = 2816
FFN_CONV = 3

SUBLANES = 8
LANES = 128
MXU_TILE = 256
VMEM_LIMIT_BYTES = 56 * 1024 * 1024

INT_MIN = np.int32(-2 ** 31)
NEG_BIG = -1e30
LOG2_E = 1.4426950408889634

TOK_TILE = 512
FFN_TILE = 512
LRU_TILE = 512
Q_TILE = 256
K_CHUNK = 256
FF_CHUNK = 1408
COUNT_ACC_ROWS = 16


def _params(sem):
    return pltpu.CompilerParams(dimension_semantics=sem, vmem_limit_bytes=VMEM_LIMIT_BYTES)


def _rms(x, g):
    ms = jnp.mean(x * x, axis=-1, keepdims=True)
    return x * lax.rsqrt(ms + RMS_EPS) * g


def _shift_rows(x, prev, j):
    r = pltpu.roll(x, j, 0)
    p = pltpu.roll(prev, j, 0)
    rid = lax.broadcasted_iota(I32, (SUBLANES, x.shape[1]), 0)
    top = jnp.where(rid < j, p, r[0:SUBLANES])
    return jnp.concatenate([top, r[SUBLANES:]], axis=0)


def _norm_proj_kernel(x_ref, g_ref, w_ref, *out_refs, splits):
    hn = _rms(x_ref[...], g_ref[...]).astype(BF16)
    z = jnp.dot(hn, w_ref[...], preferred_element_type=F32)
    off = 0
    for o_ref, n in zip(out_refs, splits):
        o_ref[...] = z[:, off:off + n].astype(o_ref.dtype)
        off += n


def _norm_proj(x, g, w, splits, dtypes, name):
    t, d = x.shape
    n = w.shape[1]
    tm = min(TOK_TILE, t)
    return pl.pallas_call(
        functools.partial(_norm_proj_kernel, splits=splits),
        grid=(t // tm,),
        in_specs=[
            pl.BlockSpec((tm, d), lambda i: (i, 0)),
            pl.BlockSpec((1, d), lambda i: (0, 0)),
            pl.BlockSpec((d, n), lambda i: (0, 0)),
        ],
        out_specs=[pl.BlockSpec((tm, s), lambda i: (i, 0)) for s in splits],
        out_shape=[jax.ShapeDtypeStruct((t, s), dt) for s, dt in zip(splits, dtypes)],
        compiler_params=_params(("arbitrary",)),
        name=name,
    )(x, g.reshape(1, d), w)


DSA_SPLITS = (ATT_W, ATT_W, ATT_W, IDX_HEADS * IDX_HD, LANES, LANES, MEM_W)
ROPE_LANES = ATT_ROT
ROPE_PACK = LANES // ROPE_LANES


def _rope(xh, c, s_lo, s_hi, half):
    return xh * c + pltpu.roll(xh, LANES - half, 1) * s_lo + pltpu.roll(xh, half, 1) * s_hi


def _dsa_proj_kernel(x_ref, g_ref, w_ref, pos_ref, inva_ref, invi_ref,
                     q_ref, k_ref, v_ref, iq_ref, ik_ref, iw_ref, mq_ref):
    hn = _rms(x_ref[...], g_ref[...]).astype(BF16)
    z = jnp.dot(hn, w_ref[...], preferred_element_type=F32)
    tm = z.shape[0]
    pos = pos_ref[...]
    rows = tm // ROPE_PACK
    lane = lax.broadcasted_iota(I32, (rows, LANES), 1)
    pp = jnp.broadcast_to(pos[(ROPE_PACK - 1) * rows:, :], (rows, LANES))
    for j in range(ROPE_PACK - 2, -1, -1):
        pp = jnp.where(lane < (j + 1) * ROPE_LANES,
                       jnp.broadcast_to(pos[j * rows:(j + 1) * rows, :], (rows, LANES)), pp)
    ang_a = pp * inva_ref[...]
    ang_i = pp * invi_ref[...]
    cos_a, sin_a, cos_i, sin_i = jnp.cos(ang_a), jnp.sin(ang_a), jnp.cos(ang_i), jnp.sin(ang_i)

    def unpack(tab, j):
        return tab if j == 0 else pltpu.roll(tab, LANES - j * ROPE_LANES, 1)

    ha = ATT_ROT // 2
    hi = IDX_ROT // 2
    m64 = lane & (IDX_HD - 1)
    in_a = lane < ROPE_LANES
    in_i = m64 < ROPE_LANES
    ca, sa_lo, sa_hi, ci, si_lo, si_hi = [], [], [], [], [], []
    for j in range(ROPE_PACK):
        c, s = unpack(cos_a, j), unpack(sin_a, j)
        ca.append(jnp.where(in_a, c, 1.0))
        sa_lo.append(jnp.where(lane < ha, -s, 0.0))
        sa_hi.append(jnp.where((lane >= ha) & (lane < 2 * ha), s, 0.0))
        c, s = unpack(cos_i, j), unpack(sin_i, j)
        c = jnp.where(lane < IDX_HD, c, pltpu.roll(c, IDX_HD, 1))
        s = jnp.where(lane < IDX_HD, s, pltpu.roll(s, IDX_HD, 1))
        ci.append(jnp.where(in_i, c, 1.0))
        si_lo.append(jnp.where(m64 < hi, -s, 0.0))
        si_hi.append(jnp.where((m64 >= hi) & (m64 < 2 * hi), s, 0.0))
    ca, sa_lo, sa_hi, ci, si_lo, si_hi = [jnp.concatenate(t, axis=0)
                                          for t in (ca, sa_lo, sa_hi, ci, si_lo, si_hi)]

    off = 0
    for h in range(ATT_HEADS):
        sl = slice(off + h * ATT_HD, off + (h + 1) * ATT_HD)
        qh = _rope(z[:, sl], ca, sa_lo, sa_hi, ha) * (ATT_HD ** -0.5 * LOG2_E)
        q_ref[:, h * ATT_HD:(h + 1) * ATT_HD] = qh.astype(q_ref.dtype)
    off += ATT_W
    for h in range(ATT_HEADS):
        sl = slice(off + h * ATT_HD, off + (h + 1) * ATT_HD)
        k_ref[:, h * ATT_HD:(h + 1) * ATT_HD] = _rope(z[:, sl], ca, sa_lo, sa_hi, ha).astype(k_ref.dtype)
    off += ATT_W
    for j in range(tm // K_CHUNK):
        v_ref[j] = z[j * K_CHUNK:(j + 1) * K_CHUNK, off:off + ATT_W].T.astype(v_ref.dtype)
    off += ATT_W
    for p in range(IDX_HEADS * IDX_HD // LANES):
        sl = slice(off + p * LANES, off + (p + 1) * LANES)
        iq_ref[:, p * LANES:(p + 1) * LANES] = _rope(z[:, sl], ci, si_lo, si_hi, hi).astype(iq_ref.dtype)
    off += IDX_HEADS * IDX_HD
    ik_ref[...] = _rope(z[:, off:off + LANES], ci, si_lo, si_hi, hi).astype(ik_ref.dtype)
    off += LANES
    iw_ref[...] = z[:, off:off + LANES].astype(iw_ref.dtype)
    off += LANES
    mq_ref[...] = z[:, off:off + MEM_W].astype(mq_ref.dtype)


def _dsa_proj(x, g, w, pos, inva, invi):
    t, d = x.shape
    n = w.shape[1]
    tm = TOK_TILE
    dtypes = (BF16, BF16, BF16, BF16, BF16, F32, BF16)
    return pl.pallas_call(
        _dsa_proj_kernel,
        grid=(t // tm,),
        in_specs=[
            pl.BlockSpec((tm, d), lambda i: (i, 0)),
            pl.BlockSpec((1, d), lambda i: (0, 0)),
            pl.BlockSpec((d, n), lambda i: (0, 0)),
            pl.BlockSpec((tm, 1), lambda i: (i, 0)),
            pl.BlockSpec((1, LANES), lambda i: (0, 0)),
            pl.BlockSpec((1, LANES), lambda i: (0, 0)),
        ],
        out_specs=[pl.BlockSpec((tm // K_CHUNK, ATT_W, K_CHUNK), lambda i: (i, 0, 0)) if j == 2
                   else pl.BlockSpec((tm, s), lambda i: (i, 0)) for j, s in enumerate(DSA_SPLITS)],
        out_shape=[jax.ShapeDtypeStruct((t // K_CHUNK, ATT_W, K_CHUNK), dt) if j == 2
                   else jax.ShapeDtypeStruct((t, s), dt) for j, (s, dt) in enumerate(zip(DSA_SPLITS, dtypes))],
        compiler_params=_params(("arbitrary",)),
        name="dsa_proj",
    )(x, g.reshape(1, d), w, pos, inva, invi)


def _lru_kernel(xb_ref, gb_ref, cw_ref, cb_ref, wa_ref, ba_ref, wx_ref, bx_ref, lam_ref, y_ref,
                prev_ref, hc_ref, a_s, b_s):
    ts, c = xb_ref.shape

    @pl.when(pl.program_id(1) == 0)
    def _():
        prev_ref[...] = jnp.zeros_like(prev_ref)
        hc_ref[...] = jnp.zeros_like(hc_ref)

    x = xb_ref[...]
    prev = prev_ref[...]
    xc = cb_ref[...] + cw_ref[3:4, :] * x
    for j in range(1, LRU_CONV):
        xc = xc + cw_ref[LRU_CONV - 1 - j:LRU_CONV - j, :] * _shift_rows(x, prev, j)
    prev_ref[...] = x[ts - SUBLANES:ts, :]

    xcb = xc.astype(BF16)
    ga, gx = [], []
    for t in range(c // MXU_TILE):
        blk = xcb[:, t * MXU_TILE:(t + 1) * MXU_TILE]
        ga.append(jnp.dot(blk, wa_ref[t], preferred_element_type=F32))
        gx.append(jnp.dot(blk, wx_ref[t], preferred_element_type=F32))
    r = jax.nn.sigmoid(jnp.concatenate(ga, axis=1) + ba_ref[...])
    i = jax.nn.sigmoid(jnp.concatenate(gx, axis=1) + bx_ref[...])

    nl = -lam_ref[...]
    softplus = jnp.maximum(nl, 0.0) + jnp.log1p(jnp.exp(-jnp.abs(nl)))
    log_a = r * ((-LRU_C) * softplus)
    a = jnp.exp(log_a)
    gain = jnp.sqrt(-jnp.tanh(log_a) * (a * a + 1.0))
    bt = gain * (i * xc)

    a3 = a.reshape(ts // SUBLANES, SUBLANES, c)
    b3 = bt.reshape(ts // SUBLANES, SUBLANES, c)
    rid = lax.broadcasted_iota(I32, (1, SUBLANES, c), 1)
    for d in (1, 2, 4):
        a_sh = jnp.where(rid >= d, pltpu.roll(a3, d, 1), 1.0)
        b_sh = jnp.where(rid >= d, pltpu.roll(b3, d, 1), 0.0)
        b3 = a3 * b_sh + b3
        a3 = a3 * a_sh
    a_s[...] = a3.reshape(ts, c)
    b_s[...] = b3.reshape(ts, c)

    def body(g, hc):
        r0 = pl.multiple_of(g * SUBLANES, SUBLANES)
        h = a_s[pl.ds(r0, SUBLANES), :] * hc + b_s[pl.ds(r0, SUBLANES), :]
        b_s[pl.ds(r0, SUBLANES), :] = h
        return jnp.broadcast_to(h[SUBLANES - 1:SUBLANES, :], (SUBLANES, c))

    hc_ref[...] = lax.fori_loop(0, ts // SUBLANES, body, hc_ref[...])
    y_ref[...] = (b_s[...] * jax.nn.gelu(gb_ref[...])).astype(y_ref.dtype)


def _lru_core(xb, gb, cw, cb, wa, ba, wx, bx, lam, batch, seq):
    c = LRU_W
    ts = LRU_TILE
    nt = seq // ts
    row = lambda b, j: (b * nt + j, 0)
    const2 = lambda b, j: (0, 0)
    const3 = lambda b, j: (0, 0, 0)
    return pl.pallas_call(
        _lru_kernel,
        grid=(batch, nt),
        in_specs=[
            pl.BlockSpec((ts, c), row),
            pl.BlockSpec((ts, c), row),
            pl.BlockSpec((LRU_CONV, c), const2),
            pl.BlockSpec((1, c), const2),
            pl.BlockSpec((c // MXU_TILE, MXU_TILE, MXU_TILE), const3),
            pl.BlockSpec((1, c), const2),
            pl.BlockSpec((c // MXU_TILE, MXU_TILE, MXU_TILE), const3),
            pl.BlockSpec((1, c), const2),
            pl.BlockSpec((1, c), const2),
        ],
        out_specs=pl.BlockSpec((ts, c), row),
        out_shape=jax.ShapeDtypeStruct((batch * seq, c), BF16),
        scratch_shapes=[
            pltpu.VMEM((SUBLANES, c), F32),
            pltpu.VMEM((SUBLANES, c), F32),
            pltpu.VMEM((ts, c), F32),
            pltpu.VMEM((ts, c), F32),
        ],
        compiler_params=_params(("arbitrary", "arbitrary")),
        name="lru_core",
    )(xb, gb, cw, cb.reshape(1, c), wa, ba.reshape(1, c), wx, bx.reshape(1, c), lam.reshape(1, c))


def _block_diag_tiles(w):
    per = MXU_TILE // LRU_BW
    w4 = w.reshape(LRU_BLOCKS // per, per, LRU_BW, LRU_BW)
    eye = jnp.eye(per, dtype=w.dtype)
    t = jnp.einsum('gpij,pq->gpiqj', w4, eye)
    return t.reshape(LRU_BLOCKS // per, MXU_TILE, MXU_TILE)


def _mix_out_kernel(a_ref, mq_ref, mk_ref, mv_ref, w_ref, x_ref, o_ref):
    mq = mq_ref[...]
    mk = mk_ref[...]
    mv = mv_ref[...]
    scale = MEM_HD ** -0.5
    n_a = a_ref.shape[1]
    scores = []
    for h in range(MEM_HEADS):
        sl = slice(h * MEM_HD, (h + 1) * MEM_HD)
        scores.append(lax.dot_general(mq[:, sl], mk[:, sl], (((1,), (1,)), ((), ())),
                                      preferred_element_type=F32))
    acc = jnp.dot(a_ref[...], w_ref[0:n_a, :], preferred_element_type=F32)
    heads = []
    for h in range(MEM_HEADS):
        sl = slice(h * MEM_HD, (h + 1) * MEM_HD)
        s = scores[h] * scale
        s = s - jnp.max(s, axis=-1, keepdims=True)
        e = jnp.exp(s)
        p = e / jnp.sum(e, axis=-1, keepdims=True)
        heads.append(jnp.dot(p.astype(BF16), mv[:, sl], preferred_element_type=F32).astype(BF16))
    acc = acc + jnp.dot(jnp.concatenate(heads, axis=1), w_ref[n_a:, :], preferred_element_type=F32)
    o_ref[...] = x_ref[...] + acc


def _mix_out(a, mq, memkv, layer, w_out, x, seq):
    t, d = x.shape
    tm = TOK_TILE
    per = seq // tm
    kin = w_out.shape[0]
    return pl.pallas_call(
        _mix_out_kernel,
        grid=(t // tm,),
        in_specs=[
            pl.BlockSpec((tm, a.shape[1]), lambda i: (i, 0)),
            pl.BlockSpec((tm, MEM_W), lambda i: (i, 0)),
            pl.BlockSpec((MEM_TOKENS, MEM_W), lambda i: (i // per, 2 * layer)),
            pl.BlockSpec((MEM_TOKENS, MEM_W), lambda i: (i // per, 2 * layer + 1)),
            pl.BlockSpec((kin, d), lambda i: (0, 0)),
            pl.BlockSpec((tm, d), lambda i: (i, 0)),
        ],
        out_specs=pl.BlockSpec((tm, d), lambda i: (i, 0)),
        out_shape=jax.ShapeDtypeStruct((t, d), F32),
        compiler_params=_params(("arbitrary",)),
        name="mix_out",
    )(a, mq, memkv, memkv, w_out, x)


def _ffn_kernel(x_ref, g_ref, wup_ref, cw_ref, cb_ref, wdn_ref, fg_ref, o_ref, hn_s, acc_s, uprev_s,
                *, final_norm):
    tm = x_ref.shape[0]
    nch = wdn_ref.shape[0] // FF_CHUNK

    @pl.when(pl.program_id(1) == 0)
    def _():
        uprev_s[...] = jnp.zeros_like(uprev_s)

    hn_s[...] = _rms(x_ref[...], g_ref[...]).astype(BF16)
    acc_s[...] = jnp.zeros_like(acc_s)

    def cols(idx):
        return slice(idx * FF_CHUNK, (idx + 1) * FF_CHUNK)

    def up(idx):
        return jnp.dot(hn_s[...], wup_ref[:, cols(idx)], preferred_element_type=F32)

    def conv(u, idx):
        prev = uprev_s[idx]
        y = cb_ref[:, cols(idx)] + cw_ref[2:3, cols(idx)] * u
        y = y + cw_ref[1:2, cols(idx)] * _shift_rows(u, prev, 1)
        y = y + cw_ref[0:1, cols(idx)] * _shift_rows(u, prev, 2)
        uprev_s[idx] = u[tm - SUBLANES:tm, :]
        return y

    def down(act, c):
        acc_s[...] += jnp.dot(act, wdn_ref[c * FF_CHUNK:(c + 1) * FF_CHUNK, :], preferred_element_type=F32)

    ug, uv = up(0), up(nch)
    act_prev = None
    for c in range(nch):
        if c + 1 < nch:
            ug_next, uv_next = up(c + 1), up(c + 1 + nch)
        if act_prev is not None:
            down(act_prev, c - 1)
        act_prev = (jax.nn.silu(conv(ug, c)) * conv(uv, c + nch)).astype(BF16)
        if c + 1 < nch:
            ug, uv = ug_next, uv_next
    down(act_prev, nch - 1)
    out = x_ref[...] + acc_s[...]
    if final_norm:
        out = _rms(out, fg_ref[...])
    o_ref[...] = out


def _ffn(x, g, w_up, conv_w, conv_b, w_down, layer, fg, batch, seq, final_norm):
    t, d = x.shape
    tm = FFN_TILE
    nt = seq // tm
    n2 = w_up.shape[2]
    row = lambda b, j: (b * nt + j, 0)
    const = lambda b, j: (0, 0)
    pick = lambda b, j: (layer, 0, 0)
    return pl.pallas_call(
        functools.partial(_ffn_kernel, final_norm=final_norm),
        grid=(batch, nt),
        in_specs=[
            pl.BlockSpec((tm, d), row),
            pl.BlockSpec((1, d), const),
            pl.BlockSpec((None,) + w_up.shape[1:], pick),
            pl.BlockSpec((None,) + conv_w.shape[1:], pick),
            pl.BlockSpec((None, 1, n2), pick),
            pl.BlockSpec((None,) + w_down.shape[1:], pick),
            pl.BlockSpec((1, d), const),
        ],
        out_specs=pl.BlockSpec((tm, d), row),
        out_shape=jax.ShapeDtypeStruct((t, d), F32),
        scratch_shapes=[
            pltpu.VMEM((tm, d), BF16),
            pltpu.VMEM((tm, d), F32),
            pltpu.VMEM((n2 // FF_CHUNK, SUBLANES, FF_CHUNK), F32),
        ],
        compiler_params=_params(("arbitrary", "arbitrary")),
        name="ffn",
    )(x, g.reshape(1, d), w_up, conv_w, conv_b.reshape(conv_b.shape[0], 1, n2), w_down, fg.reshape(1, d))


def _dsa_attn_kernel(ik_ref, iq_ref, iw_ref, q_ref, k_ref, vt_ref, o_ref,
                     keys_s, m_s, l_s, acc_s, p_s, alpha_s):
    tq = q_ref.shape[0]
    kc_n = K_CHUNK
    qi = pl.program_id(1)
    nk = qi + 1
    idx_bits = int(keys_s.shape[0]).bit_length()

    iw_t = iw_ref[...].T[0:IDX_HEADS, :] * ((IDX_HEADS ** -0.5) * (IDX_HD ** -0.5))
    iq = iq_ref[...]
    lane = lax.broadcasted_iota(I32, (kc_n, LANES), 1)
    krow = lax.broadcasted_iota(I32, (kc_n, tq), 0)
    qcol = lax.broadcasted_iota(I32, (kc_n, tq), 1)

    def score_body(kc, carry):
        r0 = pl.multiple_of(kc * kc_n, kc_n)
        ik2 = ik_ref[pl.ds(r0, kc_n), :]
        ik_lo = jnp.where(lane < IDX_HD, ik2, jnp.zeros_like(ik2))
        ik_hi = jnp.where(lane >= IDX_HD, ik2, jnp.zeros_like(ik2))
        score = jnp.zeros((kc_n, tq), F32)
        for h in range(IDX_HEADS):
            pair = iq[:, (h // 2) * LANES:(h // 2 + 1) * LANES]
            lhs = ik_lo if h % 2 == 0 else ik_hi
            rel = lax.dot_general(lhs, pair, (((1,), (1,)), ((), ())), preferred_element_type=F32)
            score = score + iw_t[h:h + 1, :] * jnp.maximum(rel, 0.0)
        causal = (krow + r0) <= (qcol + qi * tq)
        keys_s[pl.ds(r0, kc_n), :] = jnp.where(causal, score, -jnp.inf)
        return carry

    lax.fori_loop(0, nk, score_body, 0)

    def count(pred_fn):
        def body(kc, part):
            r0 = pl.multiple_of(kc * kc_n, kc_n)
            ones = jnp.where(pred_fn(keys_s[pl.ds(r0, kc_n), :], r0), 1, 0).astype(I32)
            return part + jnp.sum(ones.reshape(kc_n // COUNT_ACC_ROWS, COUNT_ACC_ROWS, tq), axis=0)
        part = lax.fori_loop(0, nk, body, jnp.zeros((COUNT_ACC_ROWS, tq), I32))
        return jnp.sum(part, axis=0, keepdims=True)

    def ordered_to_f32(u):
        neg_inf_u = jnp.int32(0x007FFFFF)
        u = jnp.where((u >= 0) & (u < neg_inf_u), neg_inf_u, u)
        k = u ^ INT_MIN
        return pltpu.bitcast(k ^ ((k >> 31) & jnp.int32(0x7FFFFFFF)), F32)

    def bit_body(t, carry):
        tu, cnt_ge = carry
        cand_u = tu | lax.shift_left(jnp.int32(1), jnp.int32(31) - t)
        cand_f = ordered_to_f32(cand_u)
        cnt = count(lambda kv, r0: kv >= cand_f)
        ok = cnt >= TOPK_MAX
        return jnp.where(ok, cand_u, tu), jnp.where(ok, cnt, cnt_ge)

    zeros = jnp.zeros((1, tq), I32)
    tu, cnt_ge = lax.fori_loop(0, 32, bit_body, (zeros, zeros + nk * kc_n))
    thr = ordered_to_f32(tu)
    short = thr == -jnp.inf
    excess = jnp.where(short, 0, cnt_ge - TOPK_MAX)

    def tie_search():
        need = TOPK_MAX - count(lambda kv, r0: kv > thr)

        def jbit(t, jc):
            cand = jc | lax.shift_left(jnp.int32(1), jnp.int32(idx_bits - 1) - t)
            cnt = count(lambda kv, r0: (kv == thr) & ((krow + r0) < cand))
            return jnp.where(cnt <= need, cand, jc)
        return lax.fori_loop(0, idx_bits, jbit, jnp.zeros((1, tq), I32))

    jcut = lax.cond(jnp.max(excess) > 0, tie_search, lambda: jnp.full((1, tq), (1 << idx_bits) - 1, I32))

    def bias_body(kc, carry):
        r0 = pl.multiple_of(kc * kc_n, kc_n)
        kv = keys_s[pl.ds(r0, kc_n), :]
        sel = ((kv > thr) | ((kv == thr) & ((krow + r0) < jcut))) & (kv > -jnp.inf)
        keys_s[pl.ds(r0, kc_n), :] = jnp.where(sel, 0.0, NEG_BIG).astype(F32)
        return carry

    lax.fori_loop(0, nk, bias_body, 0)

    m_s[...] = jnp.full_like(m_s, NEG_BIG)
    l_s[...] = jnp.zeros_like(l_s)
    acc_s[...] = jnp.zeros_like(acc_s)

    def softmax_stage(kc):
        slot = kc & 1
        r0 = pl.multiple_of(kc * kc_n, kc_n)
        bias = keys_s[pl.ds(r0, kc_n), :]
        m_all = m_s[...]
        l_all = l_s[...]
        m_rows, l_rows, a_rows = [], [], []
        for h in range(ATT_HEADS):
            sl = slice(h * ATT_HD, (h + 1) * ATT_HD)
            s = lax.dot_general(k_ref[pl.ds(r0, kc_n), sl], q_ref[:, sl], (((1,), (1,)), ((), ())),
                                preferred_element_type=F32) + bias
            m_prev = m_all[h:h + 1, :]
            m_new = jnp.maximum(m_prev, jnp.max(s, axis=0, keepdims=True))
            alpha = jnp.exp2(m_prev - m_new)
            p = jnp.exp2(s - m_new)
            l_rows.append(alpha * l_all[h:h + 1, :] + jnp.sum(p, axis=0, keepdims=True))
            m_rows.append(m_new)
            a_rows.append(alpha)
            p_s[slot, h * kc_n:(h + 1) * kc_n, :] = p.astype(BF16)
        m_s[...] = jnp.concatenate(m_rows, axis=0)
        l_s[...] = jnp.concatenate(l_rows, axis=0)
        alpha_s[slot] = jnp.concatenate(a_rows, axis=0)

    def pv_stage(kc):
        slot = kc & 1
        al = alpha_s[slot]
        for h in range(ATT_HEADS):
            sl = slice(h * ATT_HD, (h + 1) * ATT_HD)
            pv = jnp.dot(vt_ref[kc, sl, :], p_s[slot, h * kc_n:(h + 1) * kc_n, :],
                         preferred_element_type=F32)
            acc_s[sl, :] = al[h:h + 1, :] * acc_s[sl, :] + pv

    def att_body(kc, carry):
        pv_stage(kc - 1)
        softmax_stage(kc)
        return carry

    softmax_stage(jnp.int32(0))
    lax.fori_loop(1, nk, att_body, 0)
    pv_stage(nk - 1)
    for h in range(ATT_HEADS):
        sl = slice(h * ATT_HD, (h + 1) * ATT_HD)
        o_ref[:, sl] = (acc_s[sl, :] / l_s[h:h + 1, :]).T.astype(o_ref.dtype)


def _dsa_attn(ik2, iq, iw, q, k, vt, batch, seq):
    tq = Q_TILE
    assert Q_TILE == K_CHUNK and seq % Q_TILE == 0
    nq = seq // tq
    nkc = seq // K_CHUNK
    r3 = lambda a: a.reshape(batch, seq, a.shape[-1])
    tile = lambda b, j: (b, j, 0)
    full = lambda b, j: (b, 0, 0)
    out = pl.pallas_call(
        _dsa_attn_kernel,
        grid=(batch, nq),
        in_specs=[
            pl.BlockSpec((None, seq, LANES), full),
            pl.BlockSpec((None, tq, IDX_HEADS * IDX_HD), tile),
            pl.BlockSpec((None, tq, LANES), tile),
            pl.BlockSpec((None, tq, ATT_W), tile),
            pl.BlockSpec((None, seq, ATT_W), full),
            pl.BlockSpec((None, nkc, ATT_W, K_CHUNK), lambda b, j: (b, 0, 0, 0)),
        ],
        out_specs=pl.BlockSpec((None, tq, ATT_W), tile),
        out_shape=jax.ShapeDtypeStruct((batch, seq, ATT_W), BF16),
        scratch_shapes=[
            pltpu.VMEM((seq, tq), F32),
            pltpu.VMEM((ATT_HEADS, tq), F32),
            pltpu.VMEM((ATT_HEADS, tq), F32),
            pltpu.VMEM((ATT_W, tq), F32),
            pltpu.VMEM((2, ATT_HEADS * K_CHUNK, tq), BF16),
            pltpu.VMEM((2, ATT_HEADS, tq), F32),
        ],
        compiler_params=_params(("arbitrary", "arbitrary")),
        name="dsa_attn",
    )(r3(ik2), r3(iq), r3(iw), r3(q), r3(k), vt.reshape(batch, nkc, ATT_W, K_CHUNK))
    return out.reshape(batch * seq, ATT_W)


def _rope_inv_rows():
    inv_a = ROPE_THETA ** (-jnp.arange(0, ATT_ROT, 2, dtype=F32) / ATT_ROT)
    inv_i = ROPE_THETA ** (-jnp.arange(0, IDX_ROT, 2, dtype=F32) / IDX_ROT)
    row_a = jnp.tile(jnp.concatenate([inv_a, inv_a]), ROPE_PACK)
    row_i = jnp.tile(jnp.concatenate([inv_i, inv_i, jnp.zeros((ROPE_LANES - IDX_ROT,), F32)]), ROPE_PACK)
    return row_a.reshape(1, LANES), row_i.reshape(1, LANES)


def kernel(x, mem, positions, norm_mix, norm_ffn, mem_norm, final_norm, w_mem_kv, w_ffn_up, ffn_conv_w,
           ffn_conv_b, w_ffn_down, lru_w_in, lru_conv_w, lru_conv_b, lru_w_a, lru_b_a, lru_w_x, lru_b_x,
           lru_lambda, lru_w_out, dsa_w_in, dsa_w_out):
    batch, seq, d = x.shape
    t = batch * seq
    xf = x.reshape(t, d)

    w_kv = jnp.concatenate([w_mem_kv[0], w_mem_kv[1]], axis=1).astype(BF16)
    (memkv,) = _norm_proj(mem.reshape(batch * MEM_TOKENS, d), mem_norm, w_kv,
                          (w_kv.shape[1],), (BF16,), "mem_kv")

    xb, gb, mq = _norm_proj(xf, norm_mix[0], lru_w_in[0].astype(BF16),
                            (LRU_W, LRU_W, MEM_W), (F32, F32, BF16), "lru_proj")
    y = _lru_core(xb, gb, lru_conv_w[0], lru_conv_b[0],
                  _block_diag_tiles(lru_w_a[0]).astype(BF16), lru_b_a[0],
                  _block_diag_tiles(lru_w_x[0]).astype(BF16), lru_b_x[0], lru_lambda[0], batch, seq)
    xf = _mix_out(y, mq, memkv, 0, lru_w_out[0].astype(BF16), xf, seq)
    w_up_bf, w_down_bf = w_ffn_up.astype(BF16), w_ffn_down.astype(BF16)
    xf = _ffn(xf, norm_ffn[0], w_up_bf, ffn_conv_w, ffn_conv_b, w_down_bf, 0, final_norm, batch, seq, False)

    w = dsa_w_in[0].astype(BF16)
    o = np.cumsum((0, ATT_W, ATT_W, ATT_W, IDX_HEADS * IDX_HD, IDX_HD, IDX_HEADS, MEM_W))
    w_ik = w[:, o[4]:o[5]]
    w_iw = jnp.pad(w[:, o[5]:o[6]], ((0, 0), (0, LANES - IDX_HEADS)))
    w_cat = jnp.concatenate([w[:, :o[4]], w_ik, w_ik, w_iw, w[:, o[6]:o[7]]], axis=1)
    inva, invi = _rope_inv_rows()
    pos = positions.astype(F32).reshape(t, 1)
    q, k, vt, iq, ik2, iw, mq = _dsa_proj(xf, norm_mix[1], w_cat, pos, inva, invi)
    att = _dsa_attn(ik2, iq, iw, q, k, vt, batch, seq)
    xf = _mix_out(att, mq, memkv, 1, dsa_w_out[0].astype(BF16), xf, seq)
    xf = _ffn(xf, norm_ffn[1], w_up_bf, ffn_conv_w, ffn_conv_b, w_down_bf, 1, final_norm, batch, seq, True)
    return xf.reshape(batch, seq, d)
```

```python
import functools

import numpy as np
import jax
import jax.numpy as jnp
from jax import lax
from jax.experimental import pallas as pl
from jax.experimental.pallas import tpu as pltpu

F32 = jnp.float32
BF16 = jnp.bfloat16
I32 = jnp.int32

D_MODEL = 1024
RMS_EPS = 1e-6
ROPE_THETA = 500000.0

LRU_W = 1024
LRU_BLOCKS = 16
LRU_BW = LRU_W // LRU_BLOCKS
LRU_CONV = 4
LRU_C = 8.0

ATT_HEADS = 8
ATT_HD = 128
ATT_W = ATT_HEADS * ATT_HD
ATT_ROT = ATT_HD // 4
IDX_HEADS = 8
IDX_HD = 64
IDX_ROT = IDX_HD // 4
TOPK_MAX = 256

MEM_TOKENS = 256
MEM_HEADS = 4
MEM_HD = 128
MEM_W = MEM_HEADS * MEM_HD

D_FF = 2816
FFN_CONV = 3

SUBLANES = 8
LANES = 128
MXU_TILE = 256
VMEM_LIMIT_BYTES = 56 * 1024 * 1024

INT_MIN = np.int32(-2 ** 31)
NEG_BIG = -1e30
LOG2_E = 1.4426950408889634

TOK_TILE = 1024
FFN_TILE = 512
LRU_TILE = 512
Q_TILE = 256
K_CHUNK = 256
FF_CHUNK = 1408
COUNT_ACC_ROWS = 16


def _params(sem):
    return pltpu.CompilerParams(dimension_semantics=sem, vmem_limit_bytes=VMEM_LIMIT_BYTES)


def _rms(x, g):
    ms = jnp.mean(x * x, axis=-1, keepdims=True)
    return x * lax.rsqrt(ms + RMS_EPS) * g


def _shift_rows(x, prev, j):
    r = pltpu.roll(x, j, 0)
    p = pltpu.roll(prev, j, 0)
    rid = lax.broadcasted_iota(I32, (SUBLANES, x.shape[1]), 0)
    top = jnp.where(rid < j, p, r[0:SUBLANES])
    return jnp.concatenate([top, r[SUBLANES:]], axis=0)


def _norm_proj_kernel(x_ref, g_ref, w_ref, *out_refs, splits):
    hn = _rms(x_ref[...], g_ref[...]).astype(BF16)
    z = jnp.dot(hn, w_ref[...], preferred_element_type=F32)
    off = 0
    for o_ref, n in zip(out_refs, splits):
        o_ref[...] = z[:, off:off + n].astype(o_ref.dtype)
        off += n


def _norm_proj(x, g, w, splits, dtypes, name):
    t, d = x.shape
    n = w.shape[1]
    tm = min(TOK_TILE, t)
    return pl.pallas_call(
        functools.partial(_norm_proj_kernel, splits=splits),
        grid=(t // tm,),
        in_specs=[
            pl.BlockSpec((tm, d), lambda i: (i, 0)),
            pl.BlockSpec((1, d), lambda i: (0, 0)),
            pl.BlockSpec((d, n), lambda i: (0, 0)),
        ],
        out_specs=[pl.BlockSpec((tm, s), lambda i: (i, 0)) for s in splits],
        out_shape=[jax.ShapeDtypeStruct((t, s), dt) for s, dt in zip(splits, dtypes)],
        compiler_params=_params(("arbitrary",)),
        name=name,
    )(x, g.reshape(1, d), w)


DSA_SPLITS = (ATT_W, ATT_W, ATT_W, IDX_HEADS * IDX_HD, LANES, LANES, MEM_W)
ROPE_LANES = ATT_ROT
ROPE_PACK = LANES // ROPE_LANES


def _rope(xh, c, s_lo, s_hi, half):
    return xh * c + pltpu.roll(xh, LANES - half, 1) * s_lo + pltpu.roll(xh, half, 1) * s_hi


def _dsa_proj_kernel(x_ref, g_ref, w_ref, pos_ref, inva_ref, invi_ref,
                     q_ref, k_ref, v_ref, iq_ref, ik_ref, iw_ref, mq_ref):
    hn = _rms(x_ref[...], g_ref[...]).astype(BF16)
    z = jnp.dot(hn, w_ref[...], preferred_element_type=F32)
    tm = z.shape[0]
    pos = pos_ref[...]
    rows = tm // ROPE_PACK
    lane = lax.broadcasted_iota(I32, (rows, LANES), 1)
    pp = jnp.broadcast_to(pos[(ROPE_PACK - 1) * rows:, :], (rows, LANES))
    for j in range(ROPE_PACK - 2, -1, -1):
        pp = jnp.where(lane < (j + 1) * ROPE_LANES,
                       jnp.broadcast_to(pos[j * rows:(j + 1) * rows, :], (rows, LANES)), pp)
    ang_a = pp * inva_ref[...]
    ang_i = pp * invi_ref[...]
    cos_a, sin_a, cos_i, sin_i = jnp.cos(ang_a), jnp.sin(ang_a), jnp.cos(ang_i), jnp.sin(ang_i)

    def unpack(tab, j):
        return tab if j == 0 else pltpu.roll(tab, LANES - j * ROPE_LANES, 1)

    ha = ATT_ROT // 2
    hi = IDX_ROT // 2
    m64 = lane & (IDX_HD - 1)
    in_a = lane < ROPE_LANES
    in_i = m64 < ROPE_LANES
    ca, sa_lo, sa_hi, ci, si_lo, si_hi = [], [], [], [], [], []
    for j in range(ROPE_PACK):
        c, s = unpack(cos_a, j), unpack(sin_a, j)
        ca.append(jnp.where(in_a, c, 1.0))
        sa_lo.append(jnp.where(lane < ha, -s, 0.0))
        sa_hi.append(jnp.where((lane >= ha) & (lane < 2 * ha), s, 0.0))
        c, s = unpack(cos_i, j), unpack(sin_i, j)
        c = jnp.where(lane < IDX_HD, c, pltpu.roll(c, IDX_HD, 1))
        s = jnp.where(lane < IDX_HD, s, pltpu.roll(s, IDX_HD, 1))
        ci.append(jnp.where(in_i, c, 1.0))
        si_lo.append(jnp.where(m64 < hi, -s, 0.0))
        si_hi.append(jnp.where((m64 >= hi) & (m64 < 2 * hi), s, 0.0))
    ca, sa_lo, sa_hi, ci, si_lo, si_hi = [jnp.concatenate(t, axis=0)
                                          for t in (ca, sa_lo, sa_hi, ci, si_lo, si_hi)]

    off = 0
    for h in range(ATT_HEADS):
        sl = slice(off + h * ATT_HD, off + (h + 1) * ATT_HD)
        qh = _rope(z[:, sl], ca, sa_lo, sa_hi, ha) * (ATT_HD ** -0.5 * LOG2_E)
        q_ref[:, h * ATT_HD:(h + 1) * ATT_HD] = qh.astype(q_ref.dtype)
    off += ATT_W
    for h in range(ATT_HEADS):
        sl = slice(off + h * ATT_HD, off + (h + 1) * ATT_HD)
        k_ref[:, h * ATT_HD:(h + 1) * ATT_HD] = _rope(z[:, sl], ca, sa_lo, sa_hi, ha).astype(k_ref.dtype)
    off += ATT_W
    for j in range(tm // K_CHUNK):
        v_ref[j] = z[j * K_CHUNK:(j + 1) * K_CHUNK, off:off + ATT_W].T.astype(v_ref.dtype)
    off += ATT_W
    for p in range(IDX_HEADS * IDX_HD // LANES):
        sl = slice(off + p * LANES, off + (p + 1) * LANES)
        iq_ref[:, p * LANES:(p + 1) * LANES] = _rope(z[:, sl], ci, si_lo, si_hi, hi).astype(iq_ref.dtype)
    off += IDX_HEADS * IDX_HD
    ik_ref[...] = _rope(z[:, off:off + LANES], ci, si_lo, si_hi, hi).astype(ik_ref.dtype)
    off += LANES
    iw_ref[...] = z[:, off:off + LANES].astype(iw_ref.dtype)
    off += LANES
    mq_ref[...] = z[:, off:off + MEM_W].astype(mq_ref.dtype)


def _dsa_proj(x, g, w, pos, inva, invi):
    t, d = x.shape
    n = w.shape[1]
    tm = TOK_TILE
    dtypes = (BF16, BF16, BF16, BF16, BF16, F32, BF16)
    return pl.pallas_call(
        _dsa_proj_kernel,
        grid=(t // tm,),
        in_specs=[
            pl.BlockSpec((tm, d), lambda i: (i, 0)),
            pl.BlockSpec((1, d), lambda i: (0, 0)),
            pl.BlockSpec((d, n), lambda i: (0, 0)),
            pl.BlockSpec((tm, 1), lambda i: (i, 0)),
            pl.BlockSpec((1, LANES), lambda i: (0, 0)),
            pl.BlockSpec((1, LANES), lambda i: (0, 0)),
        ],
        out_specs=[pl.BlockSpec((tm // K_CHUNK, ATT_W, K_CHUNK), lambda i: (i, 0, 0)) if j == 2
                   else pl.BlockSpec((tm, s), lambda i: (i, 0)) for j, s in enumerate(DSA_SPLITS)],
        out_shape=[jax.ShapeDtypeStruct((t // K_CHUNK, ATT_W, K_CHUNK), dt) if j == 2
                   else jax.ShapeDtypeStruct((t, s), dt) for j, (s, dt) in enumerate(zip(DSA_SPLITS, dtypes))],
        compiler_params=_params(("arbitrary",)),
        name="dsa_proj",
    )(x, g.reshape(1, d), w, pos, inva, invi)


def _lru_kernel(xb_ref, gb_ref, cw_ref, cb_ref, wa_ref, ba_ref, wx_ref, bx_ref, lam_ref, y_ref,
                prev_ref, hc_ref, a_s, b_s):
    ts, c = xb_ref.shape

    @pl.when(pl.program_id(1) == 0)
    def _():
        prev_ref[...] = jnp.zeros_like(prev_ref)
        hc_ref[...] = jnp.zeros_like(hc_ref)

    x = xb_ref[...]
    prev = prev_ref[...]
    xc = cb_ref[...] + cw_ref[3:4, :] * x
    for j in range(1, LRU_CONV):
        xc = xc + cw_ref[LRU_CONV - 1 - j:LRU_CONV - j, :] * _shift_rows(x, prev, j)
    prev_ref[...] = x[ts - SUBLANES:ts, :]

    xcb = xc.astype(BF16)
    ga, gx = [], []
    for t in range(c // MXU_TILE):
        blk = xcb[:, t * MXU_TILE:(t + 1) * MXU_TILE]
        ga.append(jnp.dot(blk, wa_ref[t], preferred_element_type=F32))
        gx.append(jnp.dot(blk, wx_ref[t], preferred_element_type=F32))
    r = jax.nn.sigmoid(jnp.concatenate(ga, axis=1) + ba_ref[...])
    i = jax.nn.sigmoid(jnp.concatenate(gx, axis=1) + bx_ref[...])

    nl = -lam_ref[...]
    softplus = jnp.maximum(nl, 0.0) + jnp.log1p(jnp.exp(-jnp.abs(nl)))
    log_a = r * ((-LRU_C) * softplus)
    a = jnp.exp(log_a)
    gain = jnp.sqrt(-jnp.tanh(log_a) * (a * a + 1.0))
    bt = gain * (i * xc)

    a3 = a.reshape(ts // SUBLANES, SUBLANES, c)
    b3 = bt.reshape(ts // SUBLANES, SUBLANES, c)
    rid = lax.broadcasted_iota(I32, (1, SUBLANES, c), 1)
    for d in (1, 2, 4):
        a_sh = jnp.where(rid >= d, pltpu.roll(a3, d, 1), 1.0)
        b_sh = jnp.where(rid >= d, pltpu.roll(b3, d, 1), 0.0)
        b3 = a3 * b_sh + b3
        a3 = a3 * a_sh
    a_s[...] = a3.reshape(ts, c)
    b_s[...] = b3.reshape(ts, c)

    def body(g, hc):
        r0 = pl.multiple_of(g * SUBLANES, SUBLANES)
        h = a_s[pl.ds(r0, SUBLANES), :] * hc + b_s[pl.ds(r0, SUBLANES), :]
        b_s[pl.ds(r0, SUBLANES), :] = h
        return jnp.broadcast_to(h[SUBLANES - 1:SUBLANES, :], (SUBLANES, c))

    hc_ref[...] = lax.fori_loop(0, ts // SUBLANES, body, hc_ref[...])
    y_ref[...] = (b_s[...] * jax.nn.gelu(gb_ref[...])).astype(y_ref.dtype)


def _lru_core(xb, gb, cw, cb, wa, ba, wx, bx, lam, batch, seq):
    c = LRU_W
    ts = LRU_TILE
    nt = seq // ts
    row = lambda b, j: (b * nt + j, 0)
    const2 = lambda b, j: (0, 0)
    const3 = lambda b, j: (0, 0, 0)
    return pl.pallas_call(
        _lru_kernel,
        grid=(batch, nt),
        in_specs=[
            pl.BlockSpec((ts, c), row),
            pl.BlockSpec((ts, c), row),
            pl.BlockSpec((LRU_CONV, c), const2),
            pl.BlockSpec((1, c), const2),
            pl.BlockSpec((c // MXU_TILE, MXU_TILE, MXU_TILE), const3),
            pl.BlockSpec((1, c), const2),
            pl.BlockSpec((c // MXU_TILE, MXU_TILE, MXU_TILE), const3),
            pl.BlockSpec((1, c), const2),
            pl.BlockSpec((1, c), const2),
        ],
        out_specs=pl.BlockSpec((ts, c), row),
        out_shape=jax.ShapeDtypeStruct((batch * seq, c), BF16),
        scratch_shapes=[
            pltpu.VMEM((SUBLANES, c), F32),
            pltpu.VMEM((SUBLANES, c), F32),
            pltpu.VMEM((ts, c), F32),
            pltpu.VMEM((ts, c), F32),
        ],
        compiler_params=_params(("arbitrary", "arbitrary")),
        name="lru_core",
    )(xb, gb, cw, cb.reshape(1, c), wa, ba.reshape(1, c), wx, bx.reshape(1, c), lam.reshape(1, c))


def _block_diag_tiles(w):
    per = MXU_TILE // LRU_BW
    w4 = w.reshape(LRU_BLOCKS // per, per, LRU_BW, LRU_BW)
    eye = jnp.eye(per, dtype=w.dtype)
    t = jnp.einsum('gpij,pq->gpiqj', w4, eye)
    return t.reshape(LRU_BLOCKS // per, MXU_TILE, MXU_TILE)


def _mix_out_kernel(a_ref, mq_ref, mk_ref, mv_ref, w_ref, x_ref, o_ref):
    mq = mq_ref[...]
    mk = mk_ref[...]
    mv = mv_ref[...]
    scale = MEM_HD ** -0.5
    n_a = a_ref.shape[1]
    scores = []
    for h in range(MEM_HEADS):
        sl = slice(h * MEM_HD, (h + 1) * MEM_HD)
        scores.append(lax.dot_general(mq[:, sl], mk[:, sl], (((1,), (1,)), ((), ())),
                                      preferred_element_type=F32))
    acc = jnp.dot(a_ref[...], w_ref[0:n_a, :], preferred_element_type=F32)
    heads = []
    for h in range(MEM_HEADS):
        sl = slice(h * MEM_HD, (h + 1) * MEM_HD)
        s = scores[h] * scale
        s = s - jnp.max(s, axis=-1, keepdims=True)
        e = jnp.exp(s)
        p = e / jnp.sum(e, axis=-1, keepdims=True)
        heads.append(jnp.dot(p.astype(BF16), mv[:, sl], preferred_element_type=F32).astype(BF16))
    acc = acc + jnp.dot(jnp.concatenate(heads, axis=1), w_ref[n_a:, :], preferred_element_type=F32)
    o_ref[...] = x_ref[...] + acc


def _mix_out(a, mq, memkv, layer, w_out, x, seq):
    t, d = x.shape
    tm = TOK_TILE
    per = seq // tm
    kin = w_out.shape[0]
    return pl.pallas_call(
        _mix_out_kernel,
        grid=(t // tm,),
        in_specs=[
            pl.BlockSpec((tm, a.shape[1]), lambda i: (i, 0)),
            pl.BlockSpec((tm, MEM_W), lambda i: (i, 0)),
            pl.BlockSpec((MEM_TOKENS, MEM_W), lambda i: (i // per, 2 * layer)),
            pl.BlockSpec((MEM_TOKENS, MEM_W), lambda i: (i // per, 2 * layer + 1)),
            pl.BlockSpec((kin, d), lambda i: (0, 0)),
            pl.BlockSpec((tm, d), lambda i: (i, 0)),
        ],
        out_specs=pl.BlockSpec((tm, d), lambda i: (i, 0)),
        out_shape=jax.ShapeDtypeStruct((t, d), F32),
        compiler_params=_params(("arbitrary",)),
        name="mix_out",
    )(a, mq, memkv, memkv, w_out, x)


def _ffn_kernel(x_ref, g_ref, wup_ref, cw_ref, cb_ref, wdn_ref, fg_ref, o_ref, hn_s, acc_s, uprev_s,
                *, final_norm):
    tm = x_ref.shape[0]
    nch = wdn_ref.shape[0] // FF_CHUNK

    @pl.when(pl.program_id(1) == 0)
    def _():
        uprev_s[...] = jnp.zeros_like(uprev_s)

    hn_s[...] = _rms(x_ref[...], g_ref[...]).astype(BF16)
    acc_s[...] = jnp.zeros_like(acc_s)

    def cols(idx):
        return slice(idx * FF_CHUNK, (idx + 1) * FF_CHUNK)

    def up(idx):
        return jnp.dot(hn_s[...], wup_ref[:, cols(idx)], preferred_element_type=F32)

    def conv(u, idx):
        prev = uprev_s[idx]
        y = cb_ref[:, cols(idx)] + cw_ref[2:3, cols(idx)] * u
        y = y + cw_ref[1:2, cols(idx)] * _shift_rows(u, prev, 1)
        y = y + cw_ref[0:1, cols(idx)] * _shift_rows(u, prev, 2)
        uprev_s[idx] = u[tm - SUBLANES:tm, :]
        return y

    def down(act, c):
        acc_s[...] += jnp.dot(act, wdn_ref[c * FF_CHUNK:(c + 1) * FF_CHUNK, :], preferred_element_type=F32)

    ug, uv = up(0), up(nch)
    act_prev = None
    for c in range(nch):
        if c + 1 < nch:
            ug_next, uv_next = up(c + 1), up(c + 1 + nch)
        if act_prev is not None:
            down(act_prev, c - 1)
        act_prev = (jax.nn.silu(conv(ug, c)) * conv(uv, c + nch)).astype(BF16)
        if c + 1 < nch:
            ug, uv = ug_next, uv_next
    down(act_prev, nch - 1)
    out = x_ref[...] + acc_s[...]
    if final_norm:
        out = _rms(out, fg_ref[...])
    o_ref[...] = out


def _ffn(x, g, w_up, conv_w, conv_b, w_down, layer, fg, batch, seq, final_norm):
    t, d = x.shape
    tm = FFN_TILE
    nt = seq // tm
    n2 = w_up.shape[2]
    row = lambda b, j: (b * nt + j, 0)
    const = lambda b, j: (0, 0)
    pick = lambda b, j: (layer, 0, 0)
    return pl.pallas_call(
        functools.partial(_ffn_kernel, final_norm=final_norm),
        grid=(batch, nt),
        in_specs=[
            pl.BlockSpec((tm, d), row),
            pl.BlockSpec((1, d), const),
            pl.BlockSpec((None,) + w_up.shape[1:], pick),
            pl.BlockSpec((None,) + conv_w.shape[1:], pick),
            pl.BlockSpec((None, 1, n2), pick),
            pl.BlockSpec((None,) + w_down.shape[1:], pick),
            pl.BlockSpec((1, d), const),
        ],
        out_specs=pl.BlockSpec((tm, d), row),
        out_shape=jax.ShapeDtypeStruct((t, d), F32),
        scratch_shapes=[
            pltpu.VMEM((tm, d), BF16),
            pltpu.VMEM((tm, d), F32),
            pltpu.VMEM((n2 // FF_CHUNK, SUBLANES, FF_CHUNK), F32),
        ],
        compiler_params=_params(("arbitrary", "arbitrary")),
        name="ffn",
    )(x, g.reshape(1, d), w_up, conv_w, conv_b.reshape(conv_b.shape[0], 1, n2), w_down, fg.reshape(1, d))


def _dsa_attn_kernel(ik_ref, iq_ref, iw_ref, q_ref, k_ref, vt_ref, o_ref,
                     keys_s, m_s, l_s, acc_s, p_s, alpha_s):
    tq = q_ref.shape[0]
    kc_n = K_CHUNK
    qi = pl.program_id(1)
    nk = qi + 1
    idx_bits = int(keys_s.shape[0]).bit_length()

    iw_t = iw_ref[...].T[0:IDX_HEADS, :] * ((IDX_HEADS ** -0.5) * (IDX_HD ** -0.5))
    iq = iq_ref[...]
    lane = lax.broadcasted_iota(I32, (kc_n, LANES), 1)
    krow = lax.broadcasted_iota(I32, (kc_n, tq), 0)
    qcol = lax.broadcasted_iota(I32, (kc_n, tq), 1)

    def score_body(kc, carry):
        r0 = pl.multiple_of(kc * kc_n, kc_n)
        ik2 = ik_ref[pl.ds(r0, kc_n), :]
        ik_lo = jnp.where(lane < IDX_HD, ik2, jnp.zeros_like(ik2))
        ik_hi = jnp.where(lane >= IDX_HD, ik2, jnp.zeros_like(ik2))
        score = jnp.zeros((kc_n, tq), F32)
        for h in range(IDX_HEADS):
            pair = iq[:, (h // 2) * LANES:(h // 2 + 1) * LANES]
            lhs = ik_lo if h % 2 == 0 else ik_hi
            rel = lax.dot_general(lhs, pair, (((1,), (1,)), ((), ())), preferred_element_type=F32)
            score = score + iw_t[h:h + 1, :] * jnp.maximum(rel, 0.0)
        causal = (krow + r0) <= (qcol + qi * tq)
        keys_s[pl.ds(r0, kc_n), :] = jnp.where(causal, score, -jnp.inf)
        return carry

    lax.fori_loop(0, nk, score_body, 0)

    def count(pred_fn):
        def body(kc, part):
            r0 = pl.multiple_of(kc * kc_n, kc_n)
            ones = jnp.where(pred_fn(keys_s[pl.ds(r0, kc_n), :], r0), 1, 0).astype(I32)
            return part + jnp.sum(ones.reshape(kc_n // COUNT_ACC_ROWS, COUNT_ACC_ROWS, tq), axis=0)
        part = lax.fori_loop(0, nk, body, jnp.zeros((COUNT_ACC_ROWS, tq), I32))
        return jnp.sum(part, axis=0, keepdims=True)

    def ordered_to_f32(u):
        neg_inf_u = jnp.int32(0x007FFFFF)
        u = jnp.where((u >= 0) & (u < neg_inf_u), neg_inf_u, u)
        k = u ^ INT_MIN
        return pltpu.bitcast(k ^ ((k >> 31) & jnp.int32(0x7FFFFFFF)), F32)

    def bit_body(t, carry):
        tu, cnt_ge = carry
        cand_u = tu | lax.shift_left(jnp.int32(1), jnp.int32(31) - t)
        cand_f = ordered_to_f32(cand_u)
        cnt = count(lambda kv, r0: kv >= cand_f)
        ok = cnt >= TOPK_MAX
        return jnp.where(ok, cand_u, tu), jnp.where(ok, cnt, cnt_ge)

    zeros = jnp.zeros((1, tq), I32)
    tu, cnt_ge = lax.fori_loop(0, 32, bit_body, (zeros, zeros + nk * kc_n))
    thr = ordered_to_f32(tu)
    short = thr == -jnp.inf
    excess = jnp.where(short, 0, cnt_ge - TOPK_MAX)

    def tie_search():
        need = TOPK_MAX - count(lambda kv, r0: kv > thr)

        def jbit(t, jc):
            cand = jc | lax.shift_left(jnp.int32(1), jnp.int32(idx_bits - 1) - t)
            cnt = count(lambda kv, r0: (kv == thr) & ((krow + r0) < cand))
            return jnp.where(cnt <= need, cand, jc)
        return lax.fori_loop(0, idx_bits, jbit, jnp.zeros((1, tq), I32))

    jcut = lax.cond(jnp.max(excess) > 0, tie_search, lambda: jnp.full((1, tq), (1 << idx_bits) - 1, I32))

    def bias_body(kc, carry):
        r0 = pl.multiple_of(kc * kc_n, kc_n)
        kv = keys_s[pl.ds(r0, kc_n), :]
        sel = ((kv > thr) | ((kv == thr) & ((krow + r0) < jcut))) & (kv > -jnp.inf)
        keys_s[pl.ds(r0, kc_n), :] = jnp.where(sel, 0.0, NEG_BIG).astype(F32)
        return carry

    lax.fori_loop(0, nk, bias_body, 0)

    m_s[...] = jnp.full_like(m_s, NEG_BIG)
    l_s[...] = jnp.zeros_like(l_s)
    acc_s[...] = jnp.zeros_like(acc_s)

    def softmax_stage(kc):
        slot = kc & 1
        r0 = pl.multiple_of(kc * kc_n, kc_n)
        bias = keys_s[pl.ds(r0, kc_n), :]
        m_all = m_s[...]
        l_all = l_s[...]
        m_rows, l_rows, a_rows = [], [], []
        for h in range(ATT_HEADS):
            sl = slice(h * ATT_HD, (h + 1) * ATT_HD)
            s = lax.dot_general(k_ref[pl.ds(r0, kc_n), sl], q_ref[:, sl], (((1,), (1,)), ((), ())),
                                preferred_element_type=F32) + bias
            m_prev = m_all[h:h + 1, :]
            m_new = jnp.maximum(m_prev, jnp.max(s, axis=0, keepdims=True))
            alpha = jnp.exp2(m_prev - m_new)
            p = jnp.exp2(s - m_new)
            l_rows.append(alpha * l_all[h:h + 1, :] + jnp.sum(p, axis=0, keepdims=True))
            m_rows.append(m_new)
            a_rows.append(alpha)
            p_s[slot, h * kc_n:(h + 1) * kc_n, :] = p.astype(BF16)
        m_s[...] = jnp.concatenate(m_rows, axis=0)
        l_s[...] = jnp.concatenate(l_rows, axis=0)
        alpha_s[slot] = jnp.concatenate(a_rows, axis=0)

    def pv_stage(kc):
        slot = kc & 1
        al = alpha_s[slot]
        for h in range(ATT_HEADS):
            sl = slice(h * ATT_HD, (h + 1) * ATT_HD)
            pv = jnp.dot(vt_ref[kc, sl, :], p_s[slot, h * kc_n:(h + 1) * kc_n, :],
                         preferred_element_type=F32)
            acc_s[sl, :] = al[h:h + 1, :] * acc_s[sl, :] + pv

    def att_body(kc, carry):
        pv_stage(kc - 1)
        softmax_stage(kc)
        return carry

    softmax_stage(jnp.int32(0))
    lax.fori_loop(1, nk, att_body, 0)
    pv_stage(nk - 1)
    for h in range(ATT_HEADS):
        sl = slice(h * ATT_HD, (h + 1) * ATT_HD)
        o_ref[:, sl] = (acc_s[sl, :] / l_s[h:h + 1, :]).T.astype(o_ref.dtype)


def _dsa_attn(ik2, iq, iw, q, k, vt, batch, seq):
    tq = Q_TILE
    assert Q_TILE == K_CHUNK and seq % Q_TILE == 0
    nq = seq // tq
    nkc = seq // K_CHUNK
    r3 = lambda a: a.reshape(batch, seq, a.shape[-1])
    tile = lambda b, j: (b, j, 0)
    full = lambda b, j: (b, 0, 0)
    out = pl.pallas_call(
        _dsa_attn_kernel,
        grid=(batch, nq),
        in_specs=[
            pl.BlockSpec((None, seq, LANES), full),
            pl.BlockSpec((None, tq, IDX_HEADS * IDX_HD), tile),
            pl.BlockSpec((None, tq, LANES), tile),
            pl.BlockSpec((None, tq, ATT_W), tile),
            pl.BlockSpec((None, seq, ATT_W), full),
            pl.BlockSpec((None, nkc, ATT_W, K_CHUNK), lambda b, j: (b, 0, 0, 0)),
        ],
        out_specs=pl.BlockSpec((None, tq, ATT_W), tile),
        out_shape=jax.ShapeDtypeStruct((batch, seq, ATT_W), BF16),
        scratch_shapes=[
            pltpu.VMEM((seq, tq), F32),
            pltpu.VMEM((ATT_HEADS, tq), F32),
            pltpu.VMEM((ATT_HEADS, tq), F32),
            pltpu.VMEM((ATT_W, tq), F32),
            pltpu.VMEM((2, ATT_HEADS * K_CHUNK, tq), BF16),
            pltpu.VMEM((2, ATT_HEADS, tq), F32),
        ],
        compiler_params=_params(("arbitrary", "arbitrary")),
        name="dsa_attn",
    )(r3(ik2), r3(iq), r3(iw), r3(q), r3(k), vt.reshape(batch, nkc, ATT_W, K_CHUNK))
    return out.reshape(batch * seq, ATT_W)


def _rope_inv_rows():
    inv_a = ROPE_THETA ** (-jnp.arange(0, ATT_ROT, 2, dtype=F32) / ATT_ROT)
    inv_i = ROPE_THETA ** (-jnp.arange(0, IDX_ROT, 2, dtype=F32) / IDX_ROT)
    row_a = jnp.tile(jnp.concatenate([inv_a, inv_a]), ROPE_PACK)
    row_i = jnp.tile(jnp.concatenate([inv_i, inv_i, jnp.zeros((ROPE_LANES - IDX_ROT,), F32)]), ROPE_PACK)
    return row_a.reshape(1, LANES), row_i.reshape(1, LANES)


def kernel(x, mem, positions, norm_mix, norm_ffn, mem_norm, final_norm, w_mem_kv, w_ffn_up, ffn_conv_w,
           ffn_conv_b, w_ffn_down, lru_w_in, lru_conv_w, lru_conv_b, lru_w_a, lru_b_a, lru_w_x, lru_b_x,
           lru_lambda, lru_w_out, dsa_w_in, dsa_w_out):
    batch, seq, d = x.shape
    t = batch * seq
    xf = x.reshape(t, d)

    w_kv = jnp.concatenate([w_mem_kv[0], w_mem_kv[1]], axis=1).astype(BF16)
    (memkv,) = _norm_proj(mem.reshape(batch * MEM_TOKENS, d), mem_norm, w_kv,
                          (w_kv.shape[1],), (BF16,), "mem_kv")

    xb, gb, mq = _norm_proj(xf, norm_mix[0], lru_w_in[0].astype(BF16),
                            (LRU_W, LRU_W, MEM_W), (F32, F32, BF16), "lru_proj")
    y = _lru_core(xb, gb, lru_conv_w[0], lru_conv_b[0],
                  _block_diag_tiles(lru_w_a[0]).astype(BF16), lru_b_a[0],
                  _block_diag_tiles(lru_w_x[0]).astype(BF16), lru_b_x[0], lru_lambda[0], batch, seq)
    xf = _mix_out(y, mq, memkv, 0, lru_w_out[0].astype(BF16), xf, seq)
    w_up_bf, w_down_bf = w_ffn_up.astype(BF16), w_ffn_down.astype(BF16)
    xf = _ffn(xf, norm_ffn[0], w_up_bf, ffn_conv_w, ffn_conv_b, w_down_bf, 0, final_norm, batch, seq, False)

    w = dsa_w_in[0].astype(BF16)
    o = np.cumsum((0, ATT_W, ATT_W, ATT_W, IDX_HEADS * IDX_HD, IDX_HD, IDX_HEADS, MEM_W))
    w_ik = w[:, o[4]:o[5]]
    w_iw = jnp.pad(w[:, o[5]:o[6]], ((0, 0), (0, LANES - IDX_HEADS)))
    w_cat = jnp.concatenate([w[:, :o[4]], w_ik, w_ik, w_iw, w[:, o[6]:o[7]]], axis=1)
    inva, invi = _rope_inv_rows()
    pos = positions.astype(F32).reshape(t, 1)
    q, k, vt, iq, ik2, iw, mq = _dsa_proj(xf, norm_mix[1], w_cat, pos, inva, invi)
    att = _dsa_attn(ik2, iq, iw, q, k, vt, batch, seq)
    xf = _mix_out(att, mq, memkv, 1, dsa_w_out[0].astype(BF16), xf, seq)
    xf = _ffn(xf, norm_ffn[1], w_up_bf, ffn_conv_w, ffn_conv_b, w_down_bf, 1, final_norm, batch, seq, True)
    return xf.reshape(batch, seq, d)
```
